```python
import jax, jax.numpy as jnp
from jax import lax
import numpy as np

D_MODEL = 1024
BATCH = 1
SEQ = 16384
DEPTH = 1
DEC_BATCH = 32
DEC_SEQ = 32
PAST_LEN = 2048

CHUNK = 64
N_MEM = 256
SB_HEADS = 8
SB_HEAD_DIM = 64
SB_QBLOCK = 128
CB_HEADS = 8
CB_HEAD_DIM = 64
CB_LEFT_CHUNKS = 8
REL_CLIP = 128
MEM_HEADS = 4
MEM_HEAD_DIM = 128
N_BRANCH = 3
SB_W = SB_HEADS * SB_HEAD_DIM
CB_W = CB_HEADS * CB_HEAD_DIM
MEM_W = MEM_HEADS * MEM_HEAD_DIM
D_IN = 3 * SB_W + 3 * CB_W + MEM_W + N_BRANCH * D_MODEL
N_EXPERTS = 32
TOP_K = 4
D_EXPERT = 1024
SWIGLU_LIMIT = 7.0
SWIGLU_ALPHA = 1.702
MOE_BLOCK = 128
RMS_EPS = 1e-6
NEG_INF = -1e30

kernel_name = 'streaming_hybrid_sb_chunkband_mem_moe_step'


def rmsnorm(x, g):
    xf = x.astype(jnp.float32)
    y = xf * lax.rsqrt(jnp.mean(xf * xf, axis=-1, keepdims=True) + RMS_EPS)
    return (y * g.astype(jnp.float32)).astype(x.dtype)


def _heads(t, n, d):
    return t.reshape(t.shape[:-1] + (n, d))


def _flat(t):
    return t.reshape(t.shape[:-2] + (-1,))


def attn_inputs(h, w_in):
    sizes = (SB_W, SB_W, SB_W, CB_W, CB_W, CB_W, MEM_W)
    cuts = tuple(int(c) for c in np.cumsum(sizes))
    p = jnp.split(h @ w_in, cuts, axis=-1)
    q_sb, k_sb, v_sb = (_heads(t, SB_HEADS, SB_HEAD_DIM) for t in p[0:3])
    q_cb, k_cb, v_cb = (_heads(t, CB_HEADS, CB_HEAD_DIM) for t in p[3:6])
    q_mem = _heads(p[6], MEM_HEADS, MEM_HEAD_DIM)
    return q_sb, k_sb, v_sb, q_cb, k_cb, v_cb, q_mem, p[7]


def stick_breaking(q, k, v, q_pos, k_pos):
    z = jnp.einsum('bqhd,bkhd->bhqk', q, k).astype(jnp.float32) * (SB_HEAD_DIM ** -0.5)
    causal = k_pos[None, :] < q_pos[:, None]
    log_beta = jax.nn.log_sigmoid(z)
    log_1m = jnp.where(causal, jax.nn.log_sigmoid(-z), 0.0)
    tail = lax.cumsum(log_1m, axis=3, reverse=True) - log_1m
    a = jnp.where(causal, jnp.exp(log_beta + tail), 0.0)
    return jnp.einsum('bhqk,bkhd->bqhd', a.astype(v.dtype), v)


def stick_breaking_prompt(q, k, v):
    b, s, h, d = q.shape
    nb = s // SB_QBLOCK
    qb = jnp.moveaxis(q.reshape(b, nb, SB_QBLOCK, h, d), 1, 0)
    q_pos = jnp.arange(s, dtype=jnp.int32).reshape(nb, SB_QBLOCK)
    k_pos = jnp.arange(s, dtype=jnp.int32)
    out = lax.map(lambda a: stick_breaking(a[0], k, v, a[1], k_pos), (qb, q_pos))
    return jnp.moveaxis(out, 0, 1).reshape(b, s, h, d)


def chunk_band_attn(q, k, v, rel, kvalid, rel_table):
    s = jnp.einsum('bnqhd,bnkhd->bnhqk', q, k).astype(jnp.float32) * (CB_HEAD_DIM ** -0.5)
    bias = rel_table[:, jnp.clip(rel, -REL_CLIP, REL_CLIP) + REL_CLIP].astype(jnp.float32)
    s = jnp.where(kvalid[None, :, None, None, :], s + bias[None, None], NEG_INF)
    p = jax.nn.softmax(s, axis=-1).astype(v.dtype)
    return jnp.einsum('bnhqk,bnkhd->bnqhd', p, v)


def chunk_band_prompt(q, k, v, rel_table):
    b, s, h, d = q.shape
    nc = s // CHUNK
    lft = CB_LEFT_CHUNKS
    band_len = (lft + 1) * CHUNK

    def band(t):
        tp = jnp.pad(t.reshape(b, nc, CHUNK, h, d), ((0, 0), (lft, 0), (0, 0), (0, 0), (0, 0)))
        return jnp.stack([tp[:, i:i + nc] for i in range(lft + 1)], axis=2).reshape(b, nc, band_len, h, d)

    q_loc = lft * CHUNK + jnp.arange(CHUNK, dtype=jnp.int32)
    k_loc = jnp.arange(band_len, dtype=jnp.int32)
    rel = q_loc[:, None] - k_loc[None, :]
    k_abs = (jnp.arange(nc, dtype=jnp.int32)[:, None] - lft) * CHUNK + k_loc[None, :]
    out = chunk_band_attn(q.reshape(b, nc, CHUNK, h, d), band(k), band(v), rel, k_abs >= 0, rel_table)
    return out.reshape(b, s, h, d)


def memory_kv(mem, g_mem, w_mem_kv):
    kv = rmsnorm(mem, g_mem) @ w_mem_kv
    k, v = jnp.split(kv, 2, axis=-1)
    return _heads(k, MEM_HEADS, MEM_HEAD_DIM), _heads(v, MEM_HEADS, MEM_HEAD_DIM)


def cross_attn(q, k, v):
    s = jnp.einsum('bqhd,bmhd->bhqm', q, k).astype(jnp.float32) * (MEM_HEAD_DIM ** -0.5)
    p = jax.nn.softmax(s, axis=-1).astype(v.dtype)
    return jnp.einsum('bhqm,bmhd->bqhd', p, v)


def merge_branches(o_sb, o_cb, o_mem, gate_logits, b_gate, w_br_sb, w_br_cb, w_br_mem, w_out):
    g_sb, g_cb, g_mem = jnp.split(jax.nn.sigmoid(gate_logits + b_gate), N_BRANCH, axis=-1)
    m = g_sb * (_flat(o_sb) @ w_br_sb) + g_cb * (_flat(o_cb) @ w_br_cb) + g_mem * (_flat(o_mem) @ w_br_mem)
    return m @ w_out


def moe(h, w_router, b_router, w_up, b_up, w_down, b_down):
    t, d = h.shape
    logits = (h @ w_router).astype(jnp.float32) + b_router.astype(jnp.float32)
    top_v, top_e = lax.top_k(logits, TOP_K)
    gate = jax.nn.softmax(top_v, axis=-1)
    n = t * TOP_K
    flat_e = top_e.reshape(n)
    flat_t = jnp.repeat(jnp.arange(t, dtype=jnp.int32), TOP_K)
    order = jnp.argsort(flat_e)
    se, st, sg = flat_e[order], flat_t[order], gate.reshape(n)[order]
    counts = jnp.bincount(flat_e, length=N_EXPERTS)
    starts = jnp.cumsum(counts) - counts
    padded = (counts + MOE_BLOCK - 1) // MOE_BLOCK * MOE_BLOCK
    pends = jnp.cumsum(padded)
    pstarts = pends - padded
    dest = pstarts[se] + jnp.arange(n, dtype=jnp.int32) - starts[se]
    nblk = (n + N_EXPERTS * (MOE_BLOCK - 1) + MOE_BLOCK - 1) // MOE_BLOCK
    rows = nblk * MOE_BLOCK
    tok = jnp.zeros((rows,), jnp.int32).at[dest].set(st)
    wgt = jnp.zeros((rows,), jnp.float32).at[dest].set(sg)
    blk_e = jnp.minimum(jnp.searchsorted(pends, jnp.arange(nblk, dtype=jnp.int32) * MOE_BLOCK, side='right'), N_EXPERTS - 1)

    def expert_block(args):
        t_idx, e = args
        hu = h[t_idx] @ w_up[e] + b_up[e]
        g = jnp.minimum(hu[:, 0::2], SWIGLU_LIMIT)
        u = jnp.clip(hu[:, 1::2], -SWIGLU_LIMIT, SWIGLU_LIMIT)
        act = (u + 1.0) * (g * jax.nn.sigmoid(SWIGLU_ALPHA * g))
        return act @ w_down[e] + b_down[e]

    out = lax.map(expert_block, (tok.reshape(nblk, MOE_BLOCK), blk_e))
    out = out.reshape(rows, d).astype(jnp.float32) * wgt[:, None]
    return jax.ops.segment_sum(out, tok, num_segments=t).astype(h.dtype)


def moe3(x, w_router, b_router, w_up, b_up, w_down, b_down):
    b, s, d = x.shape
    return moe(x.reshape(b * s, d), w_router, b_router, w_up, b_up, w_down, b_down).reshape(b, s, d)


def setup_inputs(seed: int = 0) -> dict:
    key = jax.random.key(seed)
    ks = jax.random.split(key, 32)
    f32 = jnp.float32

    def nrm(k, shape, scale):
        return jax.random.normal(k, shape, f32) * scale

    def gain(k, shape):
        return 1.0 + 0.02 * jax.random.normal(k, shape, f32)

    cb_past = min(CB_LEFT_CHUNKS * CHUNK, PAST_LEN)
    d = D_MODEL
    return {
        'x_prompt': nrm(ks[0], (BATCH, SEQ, d), 1.0),
        'x_sample': nrm(ks[1], (DEC_BATCH, DEC_SEQ, d), 1.0),
        'mem_prompt': nrm(ks[2], (BATCH, N_MEM, d), 1.0),
        'cache_sb_k': nrm(ks[3], (DEPTH, DEC_BATCH, PAST_LEN, SB_HEADS, SB_HEAD_DIM), 1.0),
        'cache_sb_v': nrm(ks[4], (DEPTH, DEC_BATCH, PAST_LEN, SB_HEADS, SB_HEAD_DIM), 1.0),
        'cache_cb_k': nrm(ks[5], (DEPTH, DEC_BATCH, cb_past, CB_HEADS, CB_HEAD_DIM), 1.0),
        'cache_cb_v': nrm(ks[6], (DEPTH, DEC_BATCH, cb_past, CB_HEADS, CB_HEAD_DIM), 1.0),
        'cache_mem_k': nrm(ks[7], (DEPTH, DEC_BATCH, N_MEM, MEM_HEADS, MEM_HEAD_DIM), 1.0),
        'cache_mem_v': nrm(ks[8], (DEPTH, DEC_BATCH, N_MEM, MEM_HEADS, MEM_HEAD_DIM), 1.0),
        'g_attn': gain(ks[9], (DEPTH, d)),
        'w_in': nrm(ks[10], (DEPTH, d, D_IN), d ** -0.5),
        'b_gate': nrm(ks[11], (DEPTH, N_BRANCH * d), 0.02),
        'rel_table': nrm(ks[12], (DEPTH, CB_HEADS, 2 * REL_CLIP + 1), 0.1),
        'g_mem': gain(ks[13], (DEPTH, d)),
        'w_mem_kv': nrm(ks[14], (DEPTH, d, 2 * MEM_W), d ** -0.5),
        'w_br_sb': nrm(ks[15], (DEPTH, SB_W, d), SB_W ** -0.5),
        'w_br_cb': nrm(ks[16], (DEPTH, CB_W, d), CB_W ** -0.5),
        'w_br_mem': nrm(ks[17], (DEPTH, MEM_W, d), MEM_W ** -0.5),
        'w_out': nrm(ks[18], (DEPTH, d, d), d ** -0.5),
        'g_ffn': gain(ks[19], (DEPTH, d)),
        'w_router': nrm(ks[20], (DEPTH, d, N_EXPERTS), d ** -0.5),
        'b_router': nrm(ks[21], (DEPTH, N_EXPERTS), 0.01),
        'w_up': nrm(ks[22], (DEPTH, N_EXPERTS, d, 2 * D_EXPERT), d ** -0.5),
        'b_up': nrm(ks[23], (DEPTH, N_EXPERTS, 2 * D_EXPERT), 0.02),
        'w_down': nrm(ks[24], (DEPTH, N_EXPERTS, D_EXPERT, d), D_EXPERT ** -0.5),
        'b_down': nrm(ks[25], (DEPTH, N_EXPERTS, d), 0.02),
        'g_final': gain(ks[26], (d,)),
    }


def reference(x_prompt, x_sample, mem_prompt, cache_sb_k, cache_sb_v, cache_cb_k, cache_cb_v,
              cache_mem_k, cache_mem_v, g_attn, w_in, b_gate, rel_table, g_mem, w_mem_kv,
              w_br_sb, w_br_cb, w_br_mem, w_out, g_ffn, w_router, b_router, w_up, b_up,
              w_down, b_down, g_final):
    yp, ys = x_prompt, x_sample
    seq = x_prompt.shape[1]
    dec_seq = x_sample.shape[1]
    past = cache_sb_k.shape[2]
    cb_past = cache_cb_k.shape[2]
    band_keep = min(CB_LEFT_CHUNKS * CHUNK, seq)
    sbk_p, sbv_p, cbk_p, cbv_p, mk_p, mv_p = [], [], [], [], [], []
    sbk_s, sbv_s, cbk_s, cbv_s = [], [], [], []
    for l in range(DEPTH):
        hp = rmsnorm(yp, g_attn[l])
        q_sb, k_sb, v_sb, q_cb, k_cb, v_cb, q_m, gl = attn_inputs(hp, w_in[l])
        o_sb = stick_breaking_prompt(q_sb, k_sb, v_sb)
        o_cb = chunk_band_prompt(q_cb, k_cb, v_cb, rel_table[l])
        mk, mv = memory_kv(mem_prompt, g_mem[l], w_mem_kv[l])
        o_m = cross_attn(q_m, mk, mv)
        yp = yp + merge_branches(o_sb, o_cb, o_m, gl, b_gate[l], w_br_sb[l], w_br_cb[l], w_br_mem[l], w_out[l])
        yp = yp + moe3(rmsnorm(yp, g_ffn[l]), w_router[l], b_router[l], w_up[l], b_up[l], w_down[l], b_down[l])
        sbk_p.append(k_sb)
        sbv_p.append(v_sb)
        cbk_p.append(k_cb[:, seq - band_keep:])
        cbv_p.append(v_cb[:, seq - band_keep:])
        mk_p.append(mk)
        mv_p.append(mv)

        hs = rmsnorm(ys, g_attn[l])
        q_sb, k_sb, v_sb, q_cb, k_cb, v_cb, q_m, gl = attn_inputs(hs, w_in[l])
        q_pos = past + jnp.arange(dec_seq, dtype=jnp.int32)
        k_all = jnp.concatenate([cache_sb_k[l], k_sb], axis=1)
        v_all = jnp.concatenate([cache_sb_v[l], v_sb], axis=1)
        o_sb = stick_breaking(q_sb, k_all, v_all, q_pos, jnp.arange(past + dec_seq, dtype=jnp.int32))
        kb = jnp.concatenate([cache_cb_k[l], k_cb], axis=1)[:, None]
        vb = jnp.concatenate([cache_cb_v[l], v_cb], axis=1)[:, None]
        k_pos = jnp.concatenate([past - cb_past + jnp.arange(cb_past, dtype=jnp.int32), q_pos])
        rel = q_pos[:, None] - k_pos[None, :]
        kvalid = jnp.ones((1, cb_past + dec_seq), dtype=bool)
        o_cb = chunk_band_attn(q_cb[:, None], kb, vb, rel, kvalid, rel_table[l])[:, 0]
        o_m = cross_attn(q_m, cache_mem_k[l], cache_mem_v[l])
        ys = ys + merge_branches(o_sb, o_cb, o_m, gl, b_gate[l], w_br_sb[l], w_br_cb[l], w_br_mem[l], w_out[l])
        ys = ys + moe3(rmsnorm(ys, g_ffn[l]), w_router[l], b_router[l], w_up[l], b_up[l], w_down[l], b_down[l])
        sbk_s.append(k_sb)
        sbv_s.append(v_sb)
        cbk_s.append(k_cb)
        cbv_s.append(v_cb)
    y_prompt = rmsnorm(yp, g_final)
    y_sample = rmsnorm(ys, g_final)
    st = lambda xs: jnp.stack(xs, axis=0)
    return (y_prompt, y_sample, st(sbk_p), st(sbv_p), st(cbk_p), st(cbv_p), st(mk_p), st(mv_p),
            st(sbk_s), st(sbv_s), st(cbk_s), st(cbv_s))
```

```python
import functools

import jax
import jax.numpy as jnp
import numpy as np
from jax import lax
from jax.experimental import pallas as pl
from jax.experimental.pallas import tpu as pltpu

F32 = jnp.float32
BF16 = jnp.bfloat16
I32 = jnp.int32

CHUNK = 64
N_MEM = 256
SB_HEADS = 8
CB_HEADS = 8
HEAD_DIM = 64
MEM_HEADS = 4
MEM_HEAD_DIM = 128
CB_LEFT_CHUNKS = 8
REL_CLIP = 128
ATT_W = 512
N_QKV_SEG = 7
N_EXPERTS = 32
TOP_K = 4
SWIGLU_LIMIT = 7.0
SWIGLU_ALPHA = 1.702
RMS_EPS = 1e-6
NEG_INF = -1e30

LANES = 128
VMEM_LIMIT = 56 * 1024 * 1024
SB_UNDERFLOW = 104.0

QK_SCALE = HEAD_DIM ** -0.5
MEM_SCALE = MEM_HEAD_DIM ** -0.5


def _cparams(sem):
    return pltpu.CompilerParams(dimension_semantics=sem, vmem_limit_bytes=VMEM_LIMIT)


def _rms(x, g):
    return x * lax.rsqrt(jnp.mean(x * x, axis=-1, keepdims=True) + RMS_EPS) * g


def _dot(a, b):
    return jnp.dot(a, b, preferred_element_type=F32)


def _dot_nt(a, b):
    return lax.dot_general(a, b, (((1,), (1,)), ((), ())), preferred_element_type=F32)


def _inproj_kernel(x_ref, g_ref, w_ref, qkv_ref, ksb_ref, vsb_ref, kcb_ref, vcb_ref, gl_ref, h_scr):
    j = pl.program_id(1)

    @pl.when(j == 0)
    def _():
        h_scr[...] = _rms(x_ref[...], g_ref[...]).astype(BF16)

    acc = _dot(h_scr[...], w_ref[...])
    f32_out = {1: ksb_ref, 2: vsb_ref, 4: kcb_ref, 5: vcb_ref}
    for seg in range(N_QKV_SEG):
        @pl.when(j == seg)
        def _(seg=seg):
            scale = QK_SCALE if seg in (0, 3) else 1.0
            qkv_ref[...] = (acc * scale).astype(BF16)
            if seg in f32_out:
                f32_out[seg][...] = acc

    @pl.when(j >= N_QKV_SEG)
    def _():
        gl_ref[...] = acc


def _inproj(x, g, w_bf16, tm):
    t, d = x.shape
    n = w_bf16.shape[1]
    nseg = n // ATT_W
    row = lambda i, j: (i, 0)
    f32_spec = pl.BlockSpec((tm, ATT_W), row)
    return pl.pallas_call(
        _inproj_kernel,
        grid=(t // tm, nseg),
        in_specs=[pl.BlockSpec((tm, d), row),
                  pl.BlockSpec((1, d), lambda i, j: (0, 0)),
                  pl.BlockSpec((d, ATT_W), lambda i, j: (0, j))],
        out_specs=[pl.BlockSpec((tm, ATT_W), lambda i, j: (i, jnp.minimum(j, N_QKV_SEG - 1))),
                   f32_spec, f32_spec, f32_spec, f32_spec,
                   pl.BlockSpec((tm, ATT_W), lambda i, j: (i, jnp.maximum(j - N_QKV_SEG, 0)))],
        out_shape=[jax.ShapeDtypeStruct((t, N_QKV_SEG * ATT_W), BF16)]
        + [jax.ShapeDtypeStruct((t, ATT_W), F32)] * 4
        + [jax.ShapeDtypeStruct((t, n - N_QKV_SEG * ATT_W), F32)],
        scratch_shapes=[pltpu.VMEM((tm, d), BF16)],
        compiler_params=_cparams(("parallel", "arbitrary")),
        name="inproj",
    )(x, g, w_bf16)


def _head_select(shape):
    return lax.broadcasted_iota(I32, shape, 1) < HEAD_DIM


def _suffix_ones(n):
    return (lax.broadcasted_iota(I32, (n, n), 0) > lax.broadcasted_iota(I32, (n, n), 1)).astype(BF16)


def _sb_block(qa, kb, vb, c, u, causal):
    z = _dot_nt(qa, kb)
    sp = jnp.maximum(z, 0.0) + jnp.log1p(jnp.exp(-jnp.abs(z)))
    l1m = -sp
    if causal is not None:
        l1m = jnp.where(causal, l1m, 0.0)
    hi = l1m.astype(BF16)
    lo = (l1m - hi.astype(F32)).astype(BF16)
    tail = _dot(hi, u) + _dot(lo, u)
    a = jnp.exp((z - sp) + tail + c)
    if causal is not None:
        a = jnp.where(causal, a, 0.0)
    pv = _dot(a.astype(BF16), vb)
    return pv, c + jnp.sum(l1m, axis=1, keepdims=True)


def _sb_kernel(q_ref, kd_ref, vd_ref, ko_ref, vo_ref, o_ref, *, bk, n_old_static):
    tq = q_ref.shape[0]
    q = q_ref[...]
    kd = kd_ref[...].astype(BF16)
    vd = vd_ref[...].astype(BF16)
    first_head = _head_select((tq, LANES))
    causal = lax.broadcasted_iota(I32, (tq, tq), 1) < lax.broadcasted_iota(I32, (tq, tq), 0)
    u_diag = _suffix_ones(tq)
    u_old = u_diag if bk == tq else _suffix_ones(bk)
    if n_old_static is None:
        n_old_blocks = pl.program_id(1) * (tq // bk)
    else:
        n_old_blocks = n_old_static // bk

    outs = []
    for head in range(2):
        qa = jnp.where(first_head if head == 0 else ~first_head, q, jnp.zeros_like(q))
        pv, c = _sb_block(qa, kd, vd, jnp.zeros((tq, 1), F32), u_diag, causal)

        def cond(carry):
            b, live, _, _ = carry
            return jnp.logical_and(b >= 0, live > 0)

        def body(carry, qa=qa):
            b, _, c, acc = carry
            start = pl.multiple_of(b * bk, bk)
            kb = ko_ref[pl.ds(start, bk), :].astype(BF16)
            vb = vo_ref[pl.ds(start, bk), :].astype(BF16)
            pv, c = _sb_block(qa, kb, vb, c, u_old, None)
            live = (jnp.max(c) > -SB_UNDERFLOW).astype(I32)
            return b - 1, live, c, acc + pv

        live0 = (jnp.max(c) > -SB_UNDERFLOW).astype(I32)
        _, _, _, acc = lax.while_loop(cond, body, (n_old_blocks - 1, live0, c, pv))
        outs.append(acc)
    o_ref[...] = jnp.where(first_head, outs[0], outs[1]).astype(o_ref.dtype)


def _sb_prompt(qkv, tq):
    s = qkv.shape[0]
    npair = SB_HEADS // 2
    return pl.pallas_call(
        functools.partial(_sb_kernel, bk=tq, n_old_static=None),
        grid=(npair, s // tq),
        in_specs=[pl.BlockSpec((tq, LANES), lambda p, i: (i, p)),
                  pl.BlockSpec((tq, LANES), lambda p, i: (i, npair + p)),
                  pl.BlockSpec((tq, LANES), lambda p, i: (i, 2 * npair + p)),
                  pl.BlockSpec((s, LANES), lambda p, i: (0, npair + p)),
                  pl.BlockSpec((s, LANES), lambda p, i: (0, 2 * npair + p))],
        out_specs=pl.BlockSpec((tq, LANES), lambda p, i: (i, p)),
        out_shape=jax.ShapeDtypeStruct((s, ATT_W), BF16),
        compiler_params=_cparams(("parallel", "arbitrary")),
        name="sb_prompt",
    )(qkv, qkv, qkv, qkv, qkv)


def _sb_sample(qkv, cache_k, cache_v, nb, lq, bk):
    past = cache_k.shape[1]
    npair = SB_HEADS // 2
    return pl.pallas_call(
        functools.partial(_sb_kernel, bk=bk, n_old_static=past),
        grid=(nb, npair),
        in_specs=[pl.BlockSpec((lq, LANES), lambda b, p: (b, p)),
                  pl.BlockSpec((lq, LANES), lambda b, p: (b, npair + p)),
                  pl.BlockSpec((lq, LANES), lambda b, p: (b, 2 * npair + p)),
                  pl.BlockSpec((None, past, LANES), lambda b, p: (b, 0, p)),
                  pl.BlockSpec((None, past, LANES), lambda b, p: (b, 0, p))],
        out_specs=pl.BlockSpec((lq, LANES), lambda b, p: (b, p)),
        out_shape=jax.ShapeDtypeStruct((nb * lq, ATT_W), BF16),
        compiler_params=_cparams(("parallel", "arbitrary")),
        name="sb_sample",
    )(qkv, qkv, qkv, cache_k, cache_v)


def _softmax_pv(s_blocks, v_blocks):
    m = functools.reduce(jnp.maximum, [jnp.max(s, axis=1, keepdims=True) for s in s_blocks])
    p_blocks = [jnp.exp(s - m) for s in s_blocks]
    denom = functools.reduce(jnp.add, [jnp.sum(p, axis=1, keepdims=True) for p in p_blocks])
    pv = functools.reduce(jnp.add, [_dot(p.astype(BF16), v) for p, v in zip(p_blocks, v_blocks)])
    return pv / denom


def _cb_kernel(q_ref, *refs, sizes, lead_blocks):
    nblk = len(sizes)
    k_refs, v_refs = refs[:nblk], refs[nblk:2 * nblk]
    bias_ref, o_ref = refs[2 * nblk], refs[2 * nblk + 1]
    tq = q_ref.shape[0]
    q = q_ref[...]
    first_head = _head_select((tq, LANES))
    ks = [r[...].astype(BF16) for r in k_refs]
    vs = [r[...].astype(BF16) for r in v_refs]
    offs = np.concatenate([[0], np.cumsum(sizes)])
    outs = []
    for head in range(2):
        qa = jnp.where(first_head if head == 0 else ~first_head, q, jnp.zeros_like(q))
        s_blocks = []
        for n in range(nblk):
            b = bias_ref[head, :, int(offs[n]):int(offs[n + 1])]
            s = jnp.where(b > 0.5 * NEG_INF, _dot_nt(qa, ks[n]) + b, NEG_INF)
            if n < lead_blocks:
                s = jnp.where(pl.program_id(1) >= lead_blocks - n, s, NEG_INF)
            s_blocks.append(s)
        outs.append(_softmax_pv(s_blocks, vs))
    o_ref[...] = jnp.where(first_head, outs[0], outs[1]).astype(o_ref.dtype)


def _rel_bias(rel_table, q_pos, k_pos):
    rel = jnp.clip(q_pos[:, None] - k_pos[None, :], -REL_CLIP, REL_CLIP) + REL_CLIP
    return rel_table[:, rel].astype(F32)


def _cb_prompt(qkv, rel_table, tq):
    s = qkv.shape[0]
    npair = CB_HEADS // 2
    lead = (CB_LEFT_CHUNKS * CHUNK) // tq
    nblk = lead + 1
    q_pos = lead * tq + jnp.arange(tq, dtype=I32)
    k_pos = jnp.arange(nblk * tq, dtype=I32)
    qc, kc = q_pos // CHUNK, k_pos // CHUNK
    band = (kc[None, :] >= qc[:, None] - CB_LEFT_CHUNKS) & (kc[None, :] <= qc[:, None])
    bias = jnp.where(band[None], _rel_bias(rel_table, q_pos, k_pos), NEG_INF)
    base = 3 * npair
    kv_specs = [pl.BlockSpec((tq, LANES), functools.partial(
        lambda p, i, back, col: (jnp.maximum(i - back, 0), col + p), back=lead - n, col=base + off))
        for off in (npair, 2 * npair) for n in range(nblk)]
    return pl.pallas_call(
        functools.partial(_cb_kernel, sizes=(tq,) * nblk, lead_blocks=lead),
        grid=(npair, s // tq),
        in_specs=[pl.BlockSpec((tq, LANES), lambda p, i: (i, base + p))] + kv_specs
        + [pl.BlockSpec((2, tq, nblk * tq), lambda p, i: (p, 0, 0))],
        out_specs=pl.BlockSpec((tq, LANES), lambda p, i: (i, p)),
        out_shape=jax.ShapeDtypeStruct((s, ATT_W), BF16),
        compiler_params=_cparams(("parallel", "arbitrary")),
        name="cb_prompt",
    )(qkv, *([qkv] * (2 * nblk)), bias)


def _cb_sample(qkv, cache_k, cache_v, rel_table, nb, lq, past):
    cb_past = cache_k.shape[1]
    npair = CB_HEADS // 2
    q_pos = past + jnp.arange(lq, dtype=I32)
    k_pos = jnp.concatenate([past - cb_past + jnp.arange(cb_past, dtype=I32), q_pos])
    bias = _rel_bias(rel_table, q_pos, k_pos)
    base = 3 * npair
    return pl.pallas_call(
        functools.partial(_cb_kernel, sizes=(cb_past, lq), lead_blocks=0),
        grid=(nb, npair),
        in_specs=[pl.BlockSpec((lq, LANES), lambda b, p: (b, base + p)),
                  pl.BlockSpec((None, cb_past, LANES), lambda b, p: (b, 0, p)),
                  pl.BlockSpec((lq, LANES), lambda b, p: (b, base + npair + p)),
                  pl.BlockSpec((None, cb_past, LANES), lambda b, p: (b, 0, p)),
                  pl.BlockSpec((lq, LANES), lambda b, p: (b, base + 2 * npair + p)),
                  pl.BlockSpec((2, lq, cb_past + lq), lambda b, p: (p, 0, 0))],
        out_specs=pl.BlockSpec((lq, LANES), lambda b, p: (b, p)),
        out_shape=jax.ShapeDtypeStruct((nb * lq, ATT_W), BF16),
        compiler_params=_cparams(("parallel", "arbitrary")),
        name="cb_sample",
    )(qkv, cache_k, qkv, cache_v, qkv, bias)


def _memkv_kernel(mem_ref, g_ref, w_ref, k_ref, v_ref):
    h = _rms(mem_ref[...], g_ref[...]).astype(BF16)
    kv = _dot(h, w_ref[...])
    half = k_ref.shape[1]
    k_ref[...] = kv[:, :half]
    v_ref[...] = kv[:, half:]


def _memkv(mem, g, w_bf16):
    n = mem.shape[0]
    half = w_bf16.shape[1] // 2
    return pl.pallas_call(
        _memkv_kernel,
        out_shape=[jax.ShapeDtypeStruct((n, half), F32)] * 2,
        compiler_params=pltpu.CompilerParams(vmem_limit_bytes=VMEM_LIMIT),
        name="memkv",
    )(mem, g, w_bf16)


def _xattn_kernel(q_ref, k_ref, v_ref, o_ref):
    q = q_ref[...]
    k = k_ref[...].astype(BF16)
    v = v_ref[...].astype(BF16)
    for h in range(MEM_HEADS):
        cols = slice(h * MEM_HEAD_DIM, (h + 1) * MEM_HEAD_DIM)
        s = _dot_nt(q[:, cols], k[:, cols]) * MEM_SCALE
        o_ref[:, cols] = _softmax_pv([s], [v[:, cols]]).astype(o_ref.dtype)


def _xattn(qkv, mem_k, mem_v, tq, per_tile_kv):
    t = qkv.shape[0]
    qcol = (N_QKV_SEG - 1)
    if per_tile_kv:
        kv_spec = pl.BlockSpec((None,) + mem_k.shape[1:], lambda i: (i, 0, 0))
    else:
        kv_spec = pl.BlockSpec(mem_k.shape, lambda i: (0, 0))
    return pl.pallas_call(
        _xattn_kernel,
        grid=(t // tq,),
        in_specs=[pl.BlockSpec((tq, ATT_W), lambda i: (i, qcol)), kv_spec, kv_spec],
        out_specs=pl.BlockSpec((tq, ATT_W), lambda i: (i, 0)),
        out_shape=jax.ShapeDtypeStruct((t, ATT_W), BF16),
        compiler_params=_cparams(("parallel",)),
        name="xattn",
    )(qkv, mem_k, mem_v)


def _merge_kernel(x_ref, osb_ref, ocb_ref, om_ref, gl_ref, bg_ref, wsb_ref, wcb_ref, wm_ref, wo_ref,
                  gf_ref, wr_ref, br_ref,
                  y_ref, hf_ref, e_ref, gate_ref, rank_ref, cnt_ref, cnt_scr):
    i = pl.program_id(0)
    tm, d = x_ref.shape

    @pl.when(i == 0)
    def _():
        cnt_scr[...] = jnp.zeros_like(cnt_scr)

    m = None
    for n, (o_ref, w_ref) in enumerate(((osb_ref, wsb_ref), (ocb_ref, wcb_ref), (om_ref, wm_ref))):
        g = jax.nn.sigmoid(gl_ref[:, n * d:(n + 1) * d] + bg_ref[:, n * d:(n + 1) * d])
        term = g * _dot(o_ref[...], w_ref[...])
        m = term if m is None else m + term
    y = x_ref[...] + _dot(m.astype(BF16), wo_ref[...])
    y_ref[...] = y
    hf = _rms(y, gf_ref[...])
    hf_ref[...] = hf
    logits = _dot(hf.astype(BF16), wr_ref[...]) + br_ref[...]

    ne = logits.shape[1]
    lane = lax.broadcasted_iota(I32, (tm, ne), 1)
    work = logits
    vals, sels, idxs = [], [], []
    for _ in range(TOP_K):
        mx = jnp.max(work, axis=1, keepdims=True)
        idx = jnp.min(jnp.where(work == mx, lane, ne), axis=1, keepdims=True)
        sel = lane == idx
        vals.append(mx)
        idxs.append(idx)
        sels.append(sel)
        work = jnp.where(sel, -jnp.inf, work)
    ex = [jnp.exp(v - vals[0]) for v in vals]
    denom = functools.reduce(jnp.add, ex)

    onehot = functools.reduce(jnp.add, [s.astype(F32) for s in sels])
    before = (lax.broadcasted_iota(I32, (tm, tm), 1) < lax.broadcasted_iota(I32, (tm, tm), 0)).astype(BF16)
    prior = _dot(before, onehot.astype(BF16)) + cnt_scr[...]
    for k in range(TOP_K):
        e_ref[:, k:k + 1] = idxs[k]
        gate_ref[:, k:k + 1] = ex[k] / denom
        rank_ref[:, k:k + 1] = jnp.sum(jnp.where(sels[k], prior, 0.0), axis=1, keepdims=True).astype(I32)
    cnt_scr[...] += jnp.sum(onehot, axis=0, keepdims=True)
    cnt_ref[...] = cnt_scr[...].astype(I32)


def _merge(x, o_sb, o_cb, o_m, gl, b_gate, w_sb, w_cb, w_m, w_out, g_ffn, w_router, b_router, tm):
    t, d = x.shape
    ne = w_router.shape[1]
    row = lambda i: (i, 0)
    full = lambda a: pl.BlockSpec(a.shape, lambda i: (0, 0))
    return pl.pallas_call(
        _merge_kernel,
        grid=(t // tm,),
        in_specs=[pl.BlockSpec((tm, d), row)] + [pl.BlockSpec((tm, ATT_W), row)] * 3
        + [pl.BlockSpec((tm, gl.shape[1]), row)]
        + [full(a) for a in (b_gate, w_sb, w_cb, w_m, w_out, g_ffn, w_router, b_router)],
        out_specs=[pl.BlockSpec((tm, d), row), pl.BlockSpec((tm, d), row)]
        + [pl.BlockSpec((tm, TOP_K), row)] * 3 + [pl.BlockSpec((1, ne), lambda i: (0, 0))],
        out_shape=[jax.ShapeDtypeStruct((t, d), F32), jax.ShapeDtypeStruct((t, d), F32),
                   jax.ShapeDtypeStruct((t, TOP_K), I32), jax.ShapeDtypeStruct((t, TOP_K), F32),
                   jax.ShapeDtypeStruct((t, TOP_K), I32), jax.ShapeDtypeStruct((1, ne), I32)],
        scratch_shapes=[pltpu.VMEM((1, ne), F32)],
        compiler_params=_cparams(("arbitrary",)),
        name="merge_router",
    )(x, o_sb, o_cb, o_m, gl, b_gate, w_sb, w_cb, w_m, w_out, g_ffn, w_router, b_router)


def _dispatch_kernel(dest_ref, hf_ref, xs_in_ref, xs_ref, sem):
    del xs_in_ref
    i = pl.program_id(0)
    tm = hf_ref.shape[0]

    def row_copy(r, k):
        dst = dest_ref[(i * tm + r) * TOP_K + k]
        return pltpu.make_async_copy(hf_ref.at[pl.ds(r, 1)], xs_ref.at[pl.ds(dst, 1)], sem)

    def start(r, carry):
        for k in range(TOP_K):
            row_copy(r, k).start()
        return carry

    def wait(r, carry):
        for k in range(TOP_K):
            row_copy(r, k).wait()
        return carry

    lax.fori_loop(0, tm, start, 0)
    lax.fori_loop(0, tm, wait, 0)


def _dispatch(dest_flat, hf, rows, tm):
    t, d = hf.shape
    xs0 = jnp.zeros((rows, d), hf.dtype)
    return pl.pallas_call(
        _dispatch_kernel,
        grid_spec=pltpu.PrefetchScalarGridSpec(
            num_scalar_prefetch=1,
            grid=(t // tm,),
            in_specs=[pl.BlockSpec((tm, d), lambda i, dest: (i, 0)),
                      pl.BlockSpec(memory_space=pl.ANY)],
            out_specs=pl.BlockSpec(memory_space=pl.ANY),
            scratch_shapes=[pltpu.SemaphoreType.DMA]),
        out_shape=jax.ShapeDtypeStruct((rows, d), hf.dtype),
        input_output_aliases={2: 0},
        compiler_params=_cparams(("arbitrary",)),
        name="moe_dispatch",
    )(dest_flat, hf, xs0)


def _expert_kernel(te_ref, nt_ref, xs_ref, wg_ref, wu_ref, wd_ref, bg_ref, bu_ref, bd_ref, ys_ref,
                   wg_scr, wu_scr, wd_scr):
    t = pl.program_id(0)
    valid = t < nt_ref[0]
    changed = jnp.logical_or(t == 0, te_ref[t] != te_ref[jnp.maximum(t - 1, 0)])

    @pl.when(jnp.logical_and(valid, changed))
    def _():
        wg_scr[...] = wg_ref[...].astype(BF16)
        wu_scr[...] = wu_ref[...].astype(BF16)
        wd_scr[...] = wd_ref[...].astype(BF16)

    @pl.when(valid)
    def _():
        x = xs_ref[...].astype(BF16)
        g = jnp.minimum(_dot(x, wg_scr[...]) + bg_ref[...], SWIGLU_LIMIT)
        u = jnp.clip(_dot(x, wu_scr[...]) + bu_ref[...], -SWIGLU_LIMIT, SWIGLU_LIMIT)
        act = (u + 1.0) * (g * jax.nn.sigmoid(SWIGLU_ALPHA * g))
        ys_ref[...] = _dot(act.astype(BF16), wd_scr[...]) + bd_ref[...]

    @pl.when(jnp.logical_not(valid))
    def _():
        ys_ref[...] = jnp.zeros_like(ys_ref)


def _experts(tile_e, n_tiles_used, xs, w_g, w_u, w_d, b_g, b_u, b_d, tr):
    rows, d = xs.shape
    f = w_g.shape[2]
    wspec = lambda a: pl.BlockSpec((None,) + a.shape[1:], lambda t, te, nt: (te[t], 0, 0))
    return pl.pallas_call(
        _expert_kernel,
        grid_spec=pltpu.PrefetchScalarGridSpec(
            num_scalar_prefetch=2,
            grid=(rows // tr,),
            in_specs=[pl.BlockSpec((tr, d), lambda t, te, nt: (t, 0))]
            + [wspec(a) for a in (w_g, w_u, w_d, b_g, b_u, b_d)],
            out_specs=pl.BlockSpec((tr, d), lambda t, te, nt: (t, 0)),
            scratch_shapes=[pltpu.VMEM((d, f), BF16), pltpu.VMEM((d, f), BF16), pltpu.VMEM((f, d), BF16)]),
        out_shape=jax.ShapeDtypeStruct((rows, d), F32),
        compiler_params=_cparams(("arbitrary",)),
        name="moe_experts",
    )(tile_e, n_tiles_used, xs, w_g, w_u, w_d, b_g, b_u, b_d)


def _combine_kernel(dest_ref, y_ref, gate_ref, gfin_ref, ys_ref, o_ref, buf, sem):
    i = pl.program_id(0)
    tm = y_ref.shape[0]

    def row_copy(r, k):
        src = dest_ref[(i * tm + r) * TOP_K + k]
        return pltpu.make_async_copy(ys_ref.at[pl.ds(src, 1)], buf.at[k, pl.ds(r, 1)], sem)

    def start(r, carry):
        for k in range(TOP_K):
            row_copy(r, k).start()
        return carry

    def wait(r, carry):
        for k in range(TOP_K):
            row_copy(r, k).wait()
        return carry

    lax.fori_loop(0, tm, start, 0)
    lax.fori_loop(0, tm, wait, 0)
    moe = functools.reduce(jnp.add, [buf[k] * gate_ref[:, k:k + 1] for k in range(TOP_K)])
    o_ref[...] = _rms(y_ref[...] + moe, gfin_ref[...])


def _combine(dest_flat, y, gate, g_final, ys, tm):
    t, d = y.shape
    return pl.pallas_call(
        _combine_kernel,
        grid_spec=pltpu.PrefetchScalarGridSpec(
            num_scalar_prefetch=1,
            grid=(t // tm,),
            in_specs=[pl.BlockSpec((tm, d), lambda i, dest: (i, 0)),
                      pl.BlockSpec((tm, TOP_K), lambda i, dest: (i, 0)),
                      pl.BlockSpec((1, d), lambda i, dest: (0, 0)),
                      pl.BlockSpec(memory_space=pl.ANY)],
            out_specs=pl.BlockSpec((tm, d), lambda i, dest: (i, 0)),
            scratch_shapes=[pltpu.VMEM((TOP_K, tm, d), F32), pltpu.SemaphoreType.DMA]),
        out_shape=jax.ShapeDtypeStruct((t, d), F32),
        compiler_params=_cparams(("arbitrary",)),
        name="moe_combine",
    )(dest_flat, y, gate, g_final, ys)


def _moe(y, hf, top_e, gate, rank, counts, w_g, w_u, w_d, b_g, b_u, b_d, g_final, tm, tr):
    t = y.shape[0]
    ne = counts.shape[1]
    n_tiles = (t * TOP_K + ne * (tr - 1) + tr - 1) // tr
    tiles_per_e = (counts[0] + tr - 1) // tr
    tile_end = jnp.cumsum(tiles_per_e)
    row_start = (tile_end - tiles_per_e) * tr
    dest = (row_start[top_e] + rank).reshape(-1)
    tile_e = jnp.minimum(jnp.searchsorted(tile_end, jnp.arange(n_tiles, dtype=I32), side='right'),
                         ne - 1).astype(I32)
    xs = _dispatch(dest, hf, n_tiles * tr, tm)
    ys = _experts(tile_e, tile_end[-1:].astype(I32), xs, w_g, w_u, w_d, b_g, b_u, b_d, tr)
    return _combine(dest, y, gate, g_final, ys, tm)


def _tile(n, pref):
    return pref if n % pref == 0 else n


def _group(x, attend, weights, tm):
    (g_attn, w_in, b_gate, w_sb, w_cb, w_m, w_out, g_ffn, w_router, b_router,
     w_g, w_u, w_d, b_g, b_u, b_d, g_final) = weights
    qkv, ksb, vsb, kcb, vcb, gl = _inproj(x, g_attn, w_in, tm)
    o_sb, o_cb, o_m = attend(qkv)
    y, hf, top_e, gate, rank, counts = _merge(
        x, o_sb, o_cb, o_m, gl, b_gate, w_sb, w_cb, w_m, w_out, g_ffn, w_router, b_router, tm)
    out = _moe(y, hf, top_e, gate, rank, counts, w_g, w_u, w_d, b_g, b_u, b_d, g_final,
               _tile(x.shape[0], 256), 256)
    return out, ksb, vsb, kcb, vcb


def kernel(x_prompt, x_sample, mem_prompt, cache_sb_k, cache_sb_v, cache_cb_k, cache_cb_v, cache_mem_k, cache_mem_v, g_attn, w_in, b_gate, rel_table, g_mem, w_mem_kv, w_br_sb, w_br_cb, w_br_mem, w_out, g_ffn, w_router, b_router, w_up, b_up, w_down, b_down, g_final):
    depth = w_in.shape[0]
    assert depth == 1, "one layer: the per-group outputs below are the final residual streams"
    bp, seq, d = x_prompt.shape
    nb, lq, _ = x_sample.shape
    assert bp == 1
    past = cache_sb_k.shape[2]
    band_keep = min(CB_LEFT_CHUNKS * CHUNK, seq)
    l = 0
    row2 = lambda a: a.reshape(1, -1)
    weights = (row2(g_attn[l]), w_in[l].astype(BF16), row2(b_gate[l]),
               w_br_sb[l].astype(BF16), w_br_cb[l].astype(BF16), w_br_mem[l].astype(BF16),
               w_out[l].astype(BF16), row2(g_ffn[l]), w_router[l].astype(BF16), row2(b_router[l]),
               w_up[l][:, :, 0::2], w_up[l][:, :, 1::2], w_down[l],
               b_up[l][:, None, 0::2], b_up[l][:, None, 1::2], b_down[l][:, None, :], row2(g_final))

    mk, mv = _memkv(mem_prompt[0], row2(g_mem[l]), w_mem_kv[l].astype(BF16))

    def attend_prompt(qkv):
        return (_sb_prompt(qkv, _tile(seq, 256)), _cb_prompt(qkv, rel_table[l], _tile(seq, 256)),
                _xattn(qkv, mk, mv, _tile(seq, 512), per_tile_kv=False))

    def attend_sample(qkv):
        flat = lambda c: c.reshape(c.shape[0], c.shape[1], -1)
        return (_sb_sample(qkv, flat(cache_sb_k[l]), flat(cache_sb_v[l]), nb, lq, _tile(past, 256)),
                _cb_sample(qkv, flat(cache_cb_k[l]), flat(cache_cb_v[l]), rel_table[l], nb, lq, past),
                _xattn(qkv, flat(cache_mem_k[l]), flat(cache_mem_v[l]), lq, per_tile_kv=True))

    yp, ksb_p, vsb_p, kcb_p, vcb_p = _group(x_prompt.reshape(seq, d), attend_prompt, weights, _tile(seq, 512))
    ys, ksb_s, vsb_s, kcb_s, vcb_s = _group(x_sample.reshape(nb * lq, d), attend_sample, weights,
                                            _tile(nb * lq, 512))

    heads = lambda a, b, n, h, hd: a.reshape(1, b, n, h, hd)
    return (yp.reshape(1, seq, d), ys.reshape(nb, lq, d),
            heads(ksb_p, 1, seq, SB_HEADS, HEAD_DIM), heads(vsb_p, 1, seq, SB_HEADS, HEAD_DIM),
            heads(kcb_p[seq - band_keep:], 1, band_keep, CB_HEADS, HEAD_DIM),
            heads(vcb_p[seq - band_keep:], 1, band_keep, CB_HEADS, HEAD_DIM),
            heads(mk, 1, N_MEM, MEM_HEADS, MEM_HEAD_DIM), heads(mv, 1, N_MEM, MEM_HEADS, MEM_HEAD_DIM),
            heads(ksb_s, nb, lq, SB_HEADS, HEAD_DIM), heads(vsb_s, nb, lq, SB_HEADS, HEAD_DIM),
            heads(kcb_s, nb, lq, CB_HEADS, HEAD_DIM), heads(vcb_s, nb, lq, CB_HEADS, HEAD_DIM))
```

```python
import functools

import jax
import jax.numpy as jnp
import numpy as np
from jax import lax
from jax.experimental import pallas as pl
from jax.experimental.pallas import tpu as pltpu

F32 = jnp.float32
BF16 = jnp.bfloat16
I32 = jnp.int32

CHUNK = 64
N_MEM = 256
SB_HEADS = 8
CB_HEADS = 8
HEAD_DIM = 64
MEM_HEADS = 4
MEM_HEAD_DIM = 128
CB_LEFT_CHUNKS = 8
REL_CLIP = 128
ATT_W = 512
N_QKV_SEG = 7
N_EXPERTS = 32
TOP_K = 4
SWIGLU_LIMIT = 7.0
SWIGLU_ALPHA = 1.702
RMS_EPS = 1e-6
NEG_INF = -1e30

LANES = 128
VMEM_LIMIT = 56 * 1024 * 1024
SB_UNDERFLOW = 104.0

QK_SCALE = HEAD_DIM ** -0.5
MEM_SCALE = MEM_HEAD_DIM ** -0.5


def _cparams(sem):
    return pltpu.CompilerParams(dimension_semantics=sem, vmem_limit_bytes=VMEM_LIMIT)


def _rms(x, g):
    return x * lax.rsqrt(jnp.mean(x * x, axis=-1, keepdims=True) + RMS_EPS) * g


def _dot(a, b):
    return jnp.dot(a, b, preferred_element_type=F32)


def _dot_nt(a, b):
    return lax.dot_general(a, b, (((1,), (1,)), ((), ())), preferred_element_type=F32)


def _inproj_kernel(x_ref, g_ref, w_ref, qkv_ref, ksb_ref, vsb_ref, kcb_ref, vcb_ref, gl_ref, h_scr):
    j = pl.program_id(1)

    @pl.when(j == 0)
    def _():
        h_scr[...] = _rms(x_ref[...], g_ref[...]).astype(BF16)

    acc = _dot(h_scr[...], w_ref[...])
    f32_out = {1: ksb_ref, 2: vsb_ref, 4: kcb_ref, 5: vcb_ref}
    for seg in range(N_QKV_SEG):
        @pl.when(j == seg)
        def _(seg=seg):
            scale = QK_SCALE if seg in (0, 3) else 1.0
            qkv_ref[...] = (acc * scale).astype(BF16)
            if seg in f32_out:
                f32_out[seg][...] = acc

    @pl.when(j >= N_QKV_SEG)
    def _():
        gl_ref[...] = acc


def _inproj(x, g, w_bf16, tm):
    t, d = x.shape
    n = w_bf16.shape[1]
    nseg = n // ATT_W
    row = lambda i, j: (i, 0)
    f32_spec = pl.BlockSpec((tm, ATT_W), row)
    return pl.pallas_call(
        _inproj_kernel,
        grid=(t // tm, nseg),
        in_specs=[pl.BlockSpec((tm, d), row),
                  pl.BlockSpec((1, d), lambda i, j: (0, 0)),
                  pl.BlockSpec((d, ATT_W), lambda i, j: (0, j))],
        out_specs=[pl.BlockSpec((tm, ATT_W), lambda i, j: (i, jnp.minimum(j, N_QKV_SEG - 1))),
                   f32_spec, f32_spec, f32_spec, f32_spec,
                   pl.BlockSpec((tm, ATT_W), lambda i, j: (i, jnp.maximum(j - N_QKV_SEG, 0)))],
        out_shape=[jax.ShapeDtypeStruct((t, N_QKV_SEG * ATT_W), BF16)]
        + [jax.ShapeDtypeStruct((t, ATT_W), F32)] * 4
        + [jax.ShapeDtypeStruct((t, n - N_QKV_SEG * ATT_W), F32)],
        scratch_shapes=[pltpu.VMEM((tm, d), BF16)],
        compiler_params=_cparams(("parallel", "arbitrary")),
        name="inproj",
    )(x, g, w_bf16)


def _head_select(shape):
    return lax.broadcasted_iota(I32, shape, 1) < HEAD_DIM


def _suffix_ones(n):
    return (lax.broadcasted_iota(I32, (n, n), 0) > lax.broadcasted_iota(I32, (n, n), 1)).astype(BF16)


def _sb_block(qa, kb, vb, c, u, causal):
    z = _dot_nt(qa, kb)
    sp = jnp.maximum(z, 0.0) + jnp.log1p(jnp.exp(-jnp.abs(z)))
    l1m = -sp
    if causal is not None:
        l1m = jnp.where(causal, l1m, 0.0)
    hi = l1m.astype(BF16)
    lo = (l1m - hi.astype(F32)).astype(BF16)
    tail = _dot(hi, u) + _dot(lo, u)
    a = jnp.exp((z - sp) + tail + c)
    if causal is not None:
        a = jnp.where(causal, a, 0.0)
    pv = _dot(a.astype(BF16), vb)
    return pv, c + jnp.sum(l1m, axis=1, keepdims=True)


def _sb_kernel(q_ref, kd_ref, vd_ref, ko_ref, vo_ref, o_ref, *, bk, n_old_static):
    tq = q_ref.shape[0]
    q = q_ref[...]
    kd = kd_ref[...].astype(BF16)
    vd = vd_ref[...].astype(BF16)
    first_head = _head_select((tq, LANES))
    causal = lax.broadcasted_iota(I32, (tq, tq), 1) < lax.broadcasted_iota(I32, (tq, tq), 0)
    u_diag = _suffix_ones(tq)
    u_old = u_diag if bk == tq else _suffix_ones(bk)
    if n_old_static is None:
        n_old_blocks = pl.program_id(1) * (tq // bk)
    else:
        n_old_blocks = n_old_static // bk

    outs = []
    for head in range(2):
        qa = jnp.where(first_head if head == 0 else ~first_head, q, jnp.zeros_like(q))
        pv, c = _sb_block(qa, kd, vd, jnp.zeros((tq, 1), F32), u_diag, causal)

        def cond(carry):
            b, live, _, _ = carry
            return jnp.logical_and(b >= 0, live > 0)

        def body(carry, qa=qa):
            b, _, c, acc = carry
            start = pl.multiple_of(b * bk, bk)
            kb = ko_ref[pl.ds(start, bk), :].astype(BF16)
            vb = vo_ref[pl.ds(start, bk), :].astype(BF16)
            pv, c = _sb_block(qa, kb, vb, c, u_old, None)
            live = (jnp.max(c) > -SB_UNDERFLOW).astype(I32)
            return b - 1, live, c, acc + pv

        live0 = (jnp.max(c) > -SB_UNDERFLOW).astype(I32)
        _, _, _, acc = lax.while_loop(cond, body, (n_old_blocks - 1, live0, c, pv))
        outs.append(acc)
    o_ref[...] = jnp.where(first_head, outs[0], outs[1]).astype(o_ref.dtype)


def _sb_prompt(qkv, tq):
    s = qkv.shape[0]
    npair = SB_HEADS // 2
    return pl.pallas_call(
        functools.partial(_sb_kernel, bk=tq, n_old_static=None),
        grid=(npair, s // tq),
        in_specs=[pl.BlockSpec((tq, LANES), lambda p, i: (i, p)),
                  pl.BlockSpec((tq, LANES), lambda p, i: (i, npair + p)),
                  pl.BlockSpec((tq, LANES), lambda p, i: (i, 2 * npair + p)),
                  pl.BlockSpec((s, LANES), lambda p, i: (0, npair + p)),
                  pl.BlockSpec((s, LANES), lambda p, i: (0, 2 * npair + p))],
        out_specs=pl.BlockSpec((tq, LANES), lambda p, i: (i, p)),
        out_shape=jax.ShapeDtypeStruct((s, ATT_W), BF16),
        compiler_params=_cparams(("parallel", "arbitrary")),
        name="sb_prompt",
    )(qkv, qkv, qkv, qkv, qkv)


def _sb_sample(qkv, cache_k, cache_v, nb, lq, bk):
    past = cache_k.shape[1]
    npair = SB_HEADS // 2
    return pl.pallas_call(
        functools.partial(_sb_kernel, bk=bk, n_old_static=past),
        grid=(nb, npair),
        in_specs=[pl.BlockSpec((lq, LANES), lambda b, p: (b, p)),
                  pl.BlockSpec((lq, LANES), lambda b, p: (b, npair + p)),
                  pl.BlockSpec((lq, LANES), lambda b, p: (b, 2 * npair + p)),
                  pl.BlockSpec((None, past, LANES), lambda b, p: (b, 0, p)),
                  pl.BlockSpec((None, past, LANES), lambda b, p: (b, 0, p))],
        out_specs=pl.BlockSpec((lq, LANES), lambda b, p: (b, p)),
        out_shape=jax.ShapeDtypeStruct((nb * lq, ATT_W), BF16),
        compiler_params=_cparams(("parallel", "arbitrary")),
        name="sb_sample",
    )(qkv, qkv, qkv, cache_k, cache_v)


def _softmax_pv(s_blocks, v_blocks):
    m = functools.reduce(jnp.maximum, [jnp.max(s, axis=1, keepdims=True) for s in s_blocks])
    p_blocks = [jnp.exp(s - m) for s in s_blocks]
    denom = functools.reduce(jnp.add, [jnp.sum(p, axis=1, keepdims=True) for p in p_blocks])
    pv = functools.reduce(jnp.add, [_dot(p.astype(BF16), v) for p, v in zip(p_blocks, v_blocks)])
    return pv / denom


def _cb_kernel(q_ref, *refs, sizes, lead_blocks):
    nblk = len(sizes)
    k_refs, v_refs = refs[:nblk], refs[nblk:2 * nblk]
    bias_ref, o_ref = refs[2 * nblk], refs[2 * nblk + 1]
    tq = q_ref.shape[0]
    q = q_ref[...]
    first_head = _head_select((tq, LANES))
    ks = [r[...].astype(BF16) for r in k_refs]
    vs = [r[...].astype(BF16) for r in v_refs]
    offs = np.concatenate([[0], np.cumsum(sizes)])
    outs = []
    for head in range(2):
        qa = jnp.where(first_head if head == 0 else ~first_head, q, jnp.zeros_like(q))
        s_blocks = []
        for n in range(nblk):
            b = bias_ref[head, :, int(offs[n]):int(offs[n + 1])]
            s = jnp.where(b > 0.5 * NEG_INF, _dot_nt(qa, ks[n]) + b, NEG_INF)
            if n < lead_blocks:
                s = jnp.where(pl.program_id(1) >= lead_blocks - n, s, NEG_INF)
            s_blocks.append(s)
        outs.append(_softmax_pv(s_blocks, vs))
    o_ref[...] = jnp.where(first_head, outs[0], outs[1]).astype(o_ref.dtype)


def _rel_bias(rel_table, lq, lk, q_off):
    m = lq + lk
    pos = jnp.arange(m, dtype=I32)
    j_minus_i = jnp.where(pos < lk, pos, pos - m)
    diag = rel_table[:, jnp.clip(q_off - j_minus_i, -REL_CLIP, REL_CLIP) + REL_CLIP].astype(F32)
    h = rel_table.shape[0]
    tiled = jnp.broadcast_to(diag[:, None, :], (h, lq, m)).reshape(h, lq * m)[:, :lq * (m - 1)]
    return tiled.reshape(h, lq, m - 1)[:, :, :lk]


def _cb_prompt(qkv, rel_table, tq):
    s = qkv.shape[0]
    npair = CB_HEADS // 2
    lead = (CB_LEFT_CHUNKS * CHUNK) // tq
    nblk = lead + 1
    q_pos = lead * tq + jnp.arange(tq, dtype=I32)
    k_pos = jnp.arange(nblk * tq, dtype=I32)
    qc, kc = q_pos // CHUNK, k_pos // CHUNK
    band = (kc[None, :] >= qc[:, None] - CB_LEFT_CHUNKS) & (kc[None, :] <= qc[:, None])
    bias = jnp.where(band[None], _rel_bias(rel_table, tq, nblk * tq, lead * tq), NEG_INF)
    base = 3 * npair
    kv_specs = [pl.BlockSpec((tq, LANES), functools.partial(
        lambda p, i, back, col: (jnp.maximum(i - back, 0), col + p), back=lead - n, col=base + off))
        for off in (npair, 2 * npair) for n in range(nblk)]
    return pl.pallas_call(
        functools.partial(_cb_kernel, sizes=(tq,) * nblk, lead_blocks=lead),
        grid=(npair, s // tq),
        in_specs=[pl.BlockSpec((tq, LANES), lambda p, i: (i, base + p))] + kv_specs
        + [pl.BlockSpec((2, tq, nblk * tq), lambda p, i: (p, 0, 0))],
        out_specs=pl.BlockSpec((tq, LANES), lambda p, i: (i, p)),
        out_shape=jax.ShapeDtypeStruct((s, ATT_W), BF16),
        compiler_params=_cparams(("parallel", "arbitrary")),
        name="cb_prompt",
    )(qkv, *([qkv] * (2 * nblk)), bias)


def _cb_sample(qkv, cache_k, cache_v, rel_table, nb, lq):
    cb_past = cache_k.shape[1]
    npair = CB_HEADS // 2
    bias = _rel_bias(rel_table, lq, cb_past + lq, cb_past)
    base = 3 * npair
    return pl.pallas_call(
        functools.partial(_cb_kernel, sizes=(cb_past, lq), lead_blocks=0),
        grid=(nb, npair),
        in_specs=[pl.BlockSpec((lq, LANES), lambda b, p: (b, base + p)),
                  pl.BlockSpec((None, cb_past, LANES), lambda b, p: (b, 0, p)),
                  pl.BlockSpec((lq, LANES), lambda b, p: (b, base + npair + p)),
                  pl.BlockSpec((None, cb_past, LANES), lambda b, p: (b, 0, p)),
                  pl.BlockSpec((lq, LANES), lambda b, p: (b, base + 2 * npair + p)),
                  pl.BlockSpec((2, lq, cb_past + lq), lambda b, p: (p, 0, 0))],
        out_specs=pl.BlockSpec((lq, LANES), lambda b, p: (b, p)),
        out_shape=jax.ShapeDtypeStruct((nb * lq, ATT_W), BF16),
        compiler_params=_cparams(("parallel", "arbitrary")),
        name="cb_sample",
    )(qkv, cache_k, qkv, cache_v, qkv, bias)


def _memkv_kernel(mem_ref, g_ref, w_ref, k_ref, v_ref):
    h = _rms(mem_ref[...], g_ref[...]).astype(BF16)
    kv = _dot(h, w_ref[...])
    half = k_ref.shape[1]
    k_ref[...] = kv[:, :half]
    v_ref[...] = kv[:, half:]


def _memkv(mem, g, w_bf16):
    n = mem.shape[0]
    half = w_bf16.shape[1] // 2
    return pl.pallas_call(
        _memkv_kernel,
        out_shape=[jax.ShapeDtypeStruct((n, half), F32)] * 2,
        compiler_params=pltpu.CompilerParams(vmem_limit_bytes=VMEM_LIMIT),
        name="memkv",
    )(mem, g, w_bf16)


def _xattn_kernel(q_ref, k_ref, v_ref, o_ref):
    q = q_ref[...]
    k = k_ref[...].astype(BF16)
    v = v_ref[...].astype(BF16)
    for h in range(MEM_HEADS):
        cols = slice(h * MEM_HEAD_DIM, (h + 1) * MEM_HEAD_DIM)
        s = _dot_nt(q[:, cols], k[:, cols]) * MEM_SCALE
        o_ref[:, cols] = _softmax_pv([s], [v[:, cols]]).astype(o_ref.dtype)


def _xattn(qkv, mem_k, mem_v, tq, per_tile_kv):
    t = qkv.shape[0]
    qcol = (N_QKV_SEG - 1)
    if per_tile_kv:
        kv_spec = pl.BlockSpec((None,) + mem_k.shape[1:], lambda i: (i, 0, 0))
    else:
        kv_spec = pl.BlockSpec(mem_k.shape, lambda i: (0, 0))
    return pl.pallas_call(
        _xattn_kernel,
        grid=(t // tq,),
        in_specs=[pl.BlockSpec((tq, ATT_W), lambda i: (i, qcol)), kv_spec, kv_spec],
        out_specs=pl.BlockSpec((tq, ATT_W), lambda i: (i, 0)),
        out_shape=jax.ShapeDtypeStruct((t, ATT_W), BF16),
        compiler_params=_cparams(("parallel",)),
        name="xattn",
    )(qkv, mem_k, mem_v)


def _merge_kernel(x_ref, osb_ref, ocb_ref, om_ref, gl_ref, bg_ref, wsb_ref, wcb_ref, wm_ref, wo_ref,
                  gf_ref, wr_ref, br_ref,
                  y_ref, hf_ref, e_ref, gate_ref, rank_ref, cnt_ref, cnt_scr):
    i = pl.program_id(0)
    tm, d = x_ref.shape

    @pl.when(i == 0)
    def _():
        cnt_scr[...] = jnp.zeros_like(cnt_scr)

    m = None
    for n, (o_ref, w_ref) in enumerate(((osb_ref, wsb_ref), (ocb_ref, wcb_ref), (om_ref, wm_ref))):
        g = jax.nn.sigmoid(gl_ref[:, n * d:(n + 1) * d] + bg_ref[:, n * d:(n + 1) * d])
        term = g * _dot(o_ref[...], w_ref[...])
        m = term if m is None else m + term
    y = x_ref[...] + _dot(m.astype(BF16), wo_ref[...])
    y_ref[...] = y
    hf = _rms(y, gf_ref[...])
    hf_ref[...] = hf
    logits = _dot(hf.astype(BF16), wr_ref[...]) + br_ref[...]

    ne = logits.shape[1]
    lane = lax.broadcasted_iota(I32, (tm, ne), 1)
    work = logits
    vals, sels, idxs = [], [], []
    for _ in range(TOP_K):
        mx = jnp.max(work, axis=1, keepdims=True)
        idx = jnp.min(jnp.where(work == mx, lane, ne), axis=1, keepdims=True)
        sel = lane == idx
        vals.append(mx)
        idxs.append(idx)
        sels.append(sel)
        work = jnp.where(sel, -jnp.inf, work)
    ex = [jnp.exp(v - vals[0]) for v in vals]
    denom = functools.reduce(jnp.add, ex)

    onehot = functools.reduce(jnp.add, [s.astype(F32) for s in sels])
    before = (lax.broadcasted_iota(I32, (tm, tm), 1) < lax.broadcasted_iota(I32, (tm, tm), 0)).astype(BF16)
    prior = _dot(before, onehot.astype(BF16)) + cnt_scr[...]
    for k in range(TOP_K):
        e_ref[:, k:k + 1] = idxs[k]
        gate_ref[:, k:k + 1] = ex[k] / denom
        rank_ref[:, k:k + 1] = jnp.sum(jnp.where(sels[k], prior, 0.0), axis=1, keepdims=True).astype(I32)
    cnt_scr[...] += jnp.sum(onehot, axis=0, keepdims=True)
    cnt_ref[...] = cnt_scr[...].astype(I32)


def _merge(x, o_sb, o_cb, o_m, gl, b_gate, w_sb, w_cb, w_m, w_out, g_ffn, w_router, b_router, tm):
    t, d = x.shape
    ne = w_router.shape[1]
    row = lambda i: (i, 0)
    full = lambda a: pl.BlockSpec(a.shape, lambda i: (0, 0))
    return pl.pallas_call(
        _merge_kernel,
        grid=(t // tm,),
        in_specs=[pl.BlockSpec((tm, d), row)] + [pl.BlockSpec((tm, ATT_W), row)] * 3
        + [pl.BlockSpec((tm, gl.shape[1]), row)]
        + [full(a) for a in (b_gate, w_sb, w_cb, w_m, w_out, g_ffn, w_router, b_router)],
        out_specs=[pl.BlockSpec((tm, d), row), pl.BlockSpec((tm, d), row)]
        + [pl.BlockSpec((tm, TOP_K), row)] * 3 + [pl.BlockSpec((1, ne), lambda i: (0, 0))],
        out_shape=[jax.ShapeDtypeStruct((t, d), F32), jax.ShapeDtypeStruct((t, d), F32),
                   jax.ShapeDtypeStruct((t, TOP_K), I32), jax.ShapeDtypeStruct((t, TOP_K), F32),
                   jax.ShapeDtypeStruct((t, TOP_K), I32), jax.ShapeDtypeStruct((1, ne), I32)],
        scratch_shapes=[pltpu.VMEM((1, ne), F32)],
        compiler_params=_cparams(("arbitrary",)),
        name="merge_router",
    )(x, o_sb, o_cb, o_m, gl, b_gate, w_sb, w_cb, w_m, w_out, g_ffn, w_router, b_router)


def _dispatch_kernel(dest_ref, hf_ref, xs_in_ref, xs_ref, sem):
    del xs_in_ref
    i = pl.program_id(0)
    tm = hf_ref.shape[0]

    def row_copy(r, k):
        dst = dest_ref[(i * tm + r) * TOP_K + k]
        return pltpu.make_async_copy(hf_ref.at[pl.ds(r, 1)], xs_ref.at[pl.ds(dst, 1)], sem)

    def start(r, carry):
        for k in range(TOP_K):
            row_copy(r, k).start()
        return carry

    def wait(r, carry):
        for k in range(TOP_K):
            row_copy(r, k).wait()
        return carry

    lax.fori_loop(0, tm, start, 0)
    lax.fori_loop(0, tm, wait, 0)


def _dispatch(dest_flat, hf, rows, tm):
    t, d = hf.shape
    xs0 = jnp.zeros((rows, d), hf.dtype)
    return pl.pallas_call(
        _dispatch_kernel,
        grid_spec=pltpu.PrefetchScalarGridSpec(
            num_scalar_prefetch=1,
            grid=(t // tm,),
            in_specs=[pl.BlockSpec((tm, d), lambda i, dest: (i, 0)),
                      pl.BlockSpec(memory_space=pl.ANY)],
            out_specs=pl.BlockSpec(memory_space=pl.ANY),
            scratch_shapes=[pltpu.SemaphoreType.DMA]),
        out_shape=jax.ShapeDtypeStruct((rows, d), hf.dtype),
        input_output_aliases={2: 0},
        compiler_params=_cparams(("arbitrary",)),
        name="moe_dispatch",
    )(dest_flat, hf, xs0)


def _expert_kernel(te_ref, nt_ref, xs_ref, wup_ref, wd_ref, bg_ref, bu_ref, bd_ref, ys_ref,
                   wg_scr, wu_scr, wd_scr):
    t = pl.program_id(0)
    valid = t < nt_ref[0]
    changed = jnp.logical_or(t == 0, te_ref[t] != te_ref[jnp.maximum(t - 1, 0)])

    @pl.when(jnp.logical_and(valid, changed))
    def _():
        grp = 2 * LANES
        src = lax.broadcasted_iota(I32, (grp, grp), 0)
        dst = lax.broadcasted_iota(I32, (grp, grp), 1)
        perm = (src == jnp.where(dst < LANES, 2 * dst, 2 * (dst - LANES) + 1)).astype(BF16)
        for n in range(wup_ref.shape[1] // grp):
            w = _dot(wup_ref[:, n * grp:(n + 1) * grp].astype(BF16), perm).astype(BF16)
            wg_scr[:, n * LANES:(n + 1) * LANES] = w[:, :LANES]
            wu_scr[:, n * LANES:(n + 1) * LANES] = w[:, LANES:]
        wd_scr[...] = wd_ref[...].astype(BF16)

    @pl.when(valid)
    def _():
        x = xs_ref[...].astype(BF16)
        g = jnp.minimum(_dot(x, wg_scr[...]) + bg_ref[...], SWIGLU_LIMIT)
        u = jnp.clip(_dot(x, wu_scr[...]) + bu_ref[...], -SWIGLU_LIMIT, SWIGLU_LIMIT)
        act = (u + 1.0) * (g * jax.nn.sigmoid(SWIGLU_ALPHA * g))
        ys_ref[...] = _dot(act.astype(BF16), wd_scr[...]) + bd_ref[...]

    @pl.when(jnp.logical_not(valid))
    def _():
        ys_ref[...] = jnp.zeros_like(ys_ref)


def _experts(tile_e, n_tiles_used, xs, w_up, w_d, b_g, b_u, b_d, tr):
    rows, d = xs.shape
    f = w_d.shape[1]
    wspec = lambda a: pl.BlockSpec((None,) + a.shape[1:], lambda t, te, nt: (te[t], 0, 0))
    return pl.pallas_call(
        _expert_kernel,
        grid_spec=pltpu.PrefetchScalarGridSpec(
            num_scalar_prefetch=2,
            grid=(rows // tr,),
            in_specs=[pl.BlockSpec((tr, d), lambda t, te, nt: (t, 0))]
            + [wspec(a) for a in (w_up, w_d, b_g, b_u, b_d)],
            out_specs=pl.BlockSpec((tr, d), lambda t, te, nt: (t, 0)),
            scratch_shapes=[pltpu.VMEM((d, f), BF16), pltpu.VMEM((d, f), BF16), pltpu.VMEM((f, d), BF16)]),
        out_shape=jax.ShapeDtypeStruct((rows, d), F32),
        compiler_params=_cparams(("arbitrary",)),
        name="moe_experts",
    )(tile_e, n_tiles_used, xs, w_up, w_d, b_g, b_u, b_d)


def _combine_kernel(dest_ref, y_ref, gate_ref, gfin_ref, ys_ref, o_ref, buf, sem):
    i = pl.program_id(0)
    tm = y_ref.shape[0]

    def row_copy(r, k):
        src = dest_ref[(i * tm + r) * TOP_K + k]
        return pltpu.make_async_copy(ys_ref.at[pl.ds(src, 1)], buf.at[k, pl.ds(r, 1)], sem)

    def start(r, carry):
        for k in range(TOP_K):
            row_copy(r, k).start()
        return carry

    def wait(r, carry):
        for k in range(TOP_K):
            row_copy(r, k).wait()
        return carry

    lax.fori_loop(0, tm, start, 0)
    lax.fori_loop(0, tm, wait, 0)
    moe = functools.reduce(jnp.add, [buf[k] * gate_ref[:, k:k + 1] for k in range(TOP_K)])
    o_ref[...] = _rms(y_ref[...] + moe, gfin_ref[...])


def _combine(dest_flat, y, gate, g_final, ys, tm):
    t, d = y.shape
    return pl.pallas_call(
        _combine_kernel,
        grid_spec=pltpu.PrefetchScalarGridSpec(
            num_scalar_prefetch=1,
            grid=(t // tm,),
            in_specs=[pl.BlockSpec((tm, d), lambda i, dest: (i, 0)),
                      pl.BlockSpec((tm, TOP_K), lambda i, dest: (i, 0)),
                      pl.BlockSpec((1, d), lambda i, dest: (0, 0)),
                      pl.BlockSpec(memory_space=pl.ANY)],
            out_specs=pl.BlockSpec((tm, d), lambda i, dest: (i, 0)),
            scratch_shapes=[pltpu.VMEM((TOP_K, tm, d), F32), pltpu.SemaphoreType.DMA]),
        out_shape=jax.ShapeDtypeStruct((t, d), F32),
        compiler_params=_cparams(("arbitrary",)),
        name="moe_combine",
    )(dest_flat, y, gate, g_final, ys)


def _moe(y, hf, top_e, gate, rank, counts, w_up, w_d, b_g, b_u, b_d, g_final, tm, tr):
    t = y.shape[0]
    ne = counts.shape[1]
    n_tiles = (t * TOP_K + ne * (tr - 1) + tr - 1) // tr
    tiles_per_e = (counts[0] + tr - 1) // tr
    eid = jnp.arange(ne, dtype=I32)
    tile_end = jnp.sum(jnp.where(eid[None, :] <= eid[:, None], tiles_per_e[None, :], 0), axis=1)
    row_start = (tile_end - tiles_per_e) * tr
    dest = (jnp.sum(jnp.where(top_e[..., None] == eid, row_start, 0), axis=-1) + rank).reshape(-1)
    tile_e = jnp.minimum(jnp.sum(tile_end[None, :] <= jnp.arange(n_tiles, dtype=I32)[:, None], axis=1),
                         ne - 1).astype(I32)
    xs = _dispatch(dest, hf, n_tiles * tr, tm)
    ys = _experts(tile_e, tile_end[-1:].astype(I32), xs, w_up, w_d, b_g, b_u, b_d, tr)
    return _combine(dest, y, gate, g_final, ys, tm)


def _tile(n, pref):
    return pref if n % pref == 0 else n


def _group(x, attend, weights, tm):
    (g_attn, w_in, b_gate, w_sb, w_cb, w_m, w_out, g_ffn, w_router, b_router,
     w_up, w_d, b_g, b_u, b_d, g_final) = weights
    qkv, ksb, vsb, kcb, vcb, gl = _inproj(x, g_attn, w_in, tm)
    o_sb, o_cb, o_m = attend(qkv)
    y, hf, top_e, gate, rank, counts = _merge(
        x, o_sb, o_cb, o_m, gl, b_gate, w_sb, w_cb, w_m, w_out, g_ffn, w_router, b_router, tm)
    out = _moe(y, hf, top_e, gate, rank, counts, w_up, w_d, b_g, b_u, b_d, g_final,
               _tile(x.shape[0], 256), 256)
    return out, ksb, vsb, kcb, vcb


def kernel(x_prompt, x_sample, mem_prompt, cache_sb_k, cache_sb_v, cache_cb_k, cache_cb_v, cache_mem_k, cache_mem_v, g_attn, w_in, b_gate, rel_table, g_mem, w_mem_kv, w_br_sb, w_br_cb, w_br_mem, w_out, g_ffn, w_router, b_router, w_up, b_up, w_down, b_down, g_final):
    depth = w_in.shape[0]
    assert depth == 1, "one layer: the per-group outputs below are the final residual streams"
    bp, seq, d = x_prompt.shape
    nb, lq, _ = x_sample.shape
    assert bp == 1
    past = cache_sb_k.shape[2]
    band_keep = min(CB_LEFT_CHUNKS * CHUNK, seq)
    l = 0
    row2 = lambda a: a.reshape(1, -1)
    weights = (row2(g_attn[l]), w_in[l].astype(BF16), row2(b_gate[l]),
               w_br_sb[l].astype(BF16), w_br_cb[l].astype(BF16), w_br_mem[l].astype(BF16),
               w_out[l].astype(BF16), row2(g_ffn[l]), w_router[l].astype(BF16), row2(b_router[l]),
               w_up[l], w_down[l],
               b_up[l][:, None, 0::2], b_up[l][:, None, 1::2], b_down[l][:, None, :], row2(g_final))

    mk, mv = _memkv(mem_prompt[0], row2(g_mem[l]), w_mem_kv[l].astype(BF16))

    def attend_prompt(qkv):
        return (_sb_prompt(qkv, _tile(seq, 256)), _cb_prompt(qkv, rel_table[l], _tile(seq, 256)),
                _xattn(qkv, mk, mv, _tile(seq, 512), per_tile_kv=False))

    def attend_sample(qkv):
        flat = lambda c: c.reshape(c.shape[0], c.shape[1], -1)
        return (_sb_sample(qkv, flat(cache_sb_k[l]), flat(cache_sb_v[l]), nb, lq, _tile(past, 256)),
                _cb_sample(qkv, flat(cache_cb_k[l]), flat(cache_cb_v[l]), rel_table[l], nb, lq),
                _xattn(qkv, flat(cache_mem_k[l]), flat(cache_mem_v[l]), lq, per_tile_kv=True))

    yp, ksb_p, vsb_p, kcb_p, vcb_p = _group(x_prompt.reshape(seq, d), attend_prompt, weights, _tile(seq, 512))
    ys, ksb_s, vsb_s, kcb_s, vcb_s = _group(x_sample.reshape(nb * lq, d), attend_sample, weights,
                                            _tile(nb * lq, 512))

    heads = lambda a, b, n, h, hd: a.reshape(1, b, n, h, hd)
    return (yp.reshape(1, seq, d), ys.reshape(nb, lq, d),
            heads(ksb_p, 1, seq, SB_HEADS, HEAD_DIM), heads(vsb_p, 1, seq, SB_HEADS, HEAD_DIM),
            heads(kcb_p[seq - band_keep:], 1, band_keep, CB_HEADS, HEAD_DIM),
            heads(vcb_p[seq - band_keep:], 1, band_keep, CB_HEADS, HEAD_DIM),
            heads(mk, 1, N_MEM, MEM_HEADS, MEM_HEAD_DIM), heads(mv, 1, N_MEM, MEM_HEADS, MEM_HEAD_DIM),
            heads(ksb_s, nb, lq, SB_HEADS, HEAD_DIM), heads(vsb_s, nb, lq, SB_HEADS, HEAD_DIM),
            heads(kcb_s, nb, lq, CB_HEADS, HEAD_DIM), heads(vcb_s, nb, lq, CB_HEADS, HEAD_DIM))
```

```python
import functools

import jax
import jax.numpy as jnp
import numpy as np
from jax import lax
from jax.experimental import pallas as pl
from jax.experimental.pallas import tpu as pltpu

F32 = jnp.float32
BF16 = jnp.bfloat16
I32 = jnp.int32

CHUNK = 64
N_MEM = 256
SB_HEADS = 8
CB_HEADS = 8
HEAD_DIM = 64
MEM_HEADS = 4
MEM_HEAD_DIM = 128
CB_LEFT_CHUNKS = 8
REL_CLIP = 128
ATT_W = 512
N_QKV_SEG = 7
N_EXPERTS = 32
TOP_K = 4
SWIGLU_LIMIT = 7.0
SWIGLU_ALPHA = 1.702
RMS_EPS = 1e-6
NEG_INF = -1e30

LANES = 128
VMEM_LIMIT = 56 * 1024 * 1024
SB_UNDERFLOW = 104.0

ROW_UNROLL = 8

QK_SCALE = HEAD_DIM ** -0.5
MEM_SCALE = MEM_HEAD_DIM ** -0.5


def _cparams(sem):
    return pltpu.CompilerParams(dimension_semantics=sem, vmem_limit_bytes=VMEM_LIMIT)


def _rms(x, g):
    return x * lax.rsqrt(jnp.mean(x * x, axis=-1, keepdims=True) + RMS_EPS) * g


def _dot(a, b):
    return jnp.dot(a, b, preferred_element_type=F32)


def _dot_nt(a, b):
    return lax.dot_general(a, b, (((1,), (1,)), ((), ())), preferred_element_type=F32)


def _inproj_kernel(x_ref, g_ref, w_ref, bg_ref, qkv_ref, ksb_ref, vsb_ref, kcb_ref, vcb_ref, gate_ref, h_scr):
    h_scr[...] = _rms(x_ref[...], g_ref[...]).astype(BF16)
    f32_out = {1: ksb_ref, 2: vsb_ref, 4: kcb_ref, 5: vcb_ref}
    for seg in range(w_ref.shape[1] // ATT_W):
        cols = slice(seg * ATT_W, (seg + 1) * ATT_W)
        acc = _dot(h_scr[...], w_ref[:, cols])
        if seg < N_QKV_SEG:
            qkv_ref[:, cols] = (acc * (QK_SCALE if seg in (0, 3) else 1.0)).astype(BF16)
            if seg in f32_out:
                f32_out[seg][...] = acc
        else:
            gcols = slice((seg - N_QKV_SEG) * ATT_W, (seg - N_QKV_SEG + 1) * ATT_W)
            gate_ref[:, gcols] = jax.nn.sigmoid(acc + bg_ref[:, gcols]).astype(BF16)


def _resident(a):
    return pl.BlockSpec(a.shape, lambda *_: (0,) * a.ndim, pipeline_mode=pl.Buffered(1))


def _inproj(x, g, w_bf16, b_gate, tm):
    t, d = x.shape
    n_gate = w_bf16.shape[1] - N_QKV_SEG * ATT_W
    row = lambda i: (i, 0)
    f32_spec = pl.BlockSpec((tm, ATT_W), row)
    return pl.pallas_call(
        _inproj_kernel,
        grid=(t // tm,),
        in_specs=[pl.BlockSpec((tm, d), row), _resident(g), _resident(w_bf16), _resident(b_gate)],
        out_specs=[pl.BlockSpec((tm, N_QKV_SEG * ATT_W), row), f32_spec, f32_spec, f32_spec, f32_spec,
                   pl.BlockSpec((tm, n_gate), row)],
        out_shape=[jax.ShapeDtypeStruct((t, N_QKV_SEG * ATT_W), BF16)]
        + [jax.ShapeDtypeStruct((t, ATT_W), F32)] * 4
        + [jax.ShapeDtypeStruct((t, n_gate), BF16)],
        scratch_shapes=[pltpu.VMEM((tm, d), BF16)],
        compiler_params=_cparams(("parallel",)),
        name="inproj",
    )(x, g, w_bf16, b_gate)


def _head_select(shape):
    return lax.broadcasted_iota(I32, shape, 1) < HEAD_DIM


def _suffix_ones(n):
    return (lax.broadcasted_iota(I32, (n, n), 0) > lax.broadcasted_iota(I32, (n, n), 1)).astype(BF16)


def _sb_block(q, kb, vb, c, u, causal=None, valid=None):
    z = _dot_nt(q, kb)
    sp = jnp.maximum(z, 0.0) + jnp.log1p(jnp.exp(-jnp.abs(z)))
    keep = causal if valid is None else (valid if causal is None else jnp.logical_and(causal, valid))
    l1m = -sp if keep is None else jnp.where(keep, -sp, 0.0)
    hi = l1m.astype(BF16)
    lo = (l1m - hi.astype(F32)).astype(BF16)
    tail = _dot(hi, u) + _dot(lo, u)
    a = jnp.exp((z - sp) + tail + c)
    if keep is not None:
        a = jnp.where(keep, a, 0.0)
    return _dot(a.astype(BF16), vb), c + jnp.sum(l1m, axis=1, keepdims=True)


def _any_live(cs):
    return (functools.reduce(jnp.maximum, [jnp.max(c) for c in cs]) > -SB_UNDERFLOW).astype(I32)


def _sb_older_blocks(first_block, fetch, step, cs, accs):
    n = len(cs)

    def cond(carry):
        return jnp.logical_and(carry[0] >= 0, carry[1] > 0)

    def body(carry):
        b, cs, accs = carry[0], carry[2:2 + n], carry[2 + n:]
        kb, vb = fetch(b)
        res = [step(s, kb, vb, cs[s]) for s in range(n)]
        cs = [r[1] for r in res]
        return (b - 1, _any_live(cs), *cs, *[acc + r[0] for acc, r in zip(accs, res)])

    out = lax.while_loop(cond, body, (first_block, _any_live(cs), *cs, *accs))
    return out[2 + n:]


def _fetch_blocks(copies):
    for c in copies:
        c.start()
    for c in copies:
        c.wait()


def _sb_prompt_kernel(q_ref, kd_ref, vd_ref, kp_ref, vp_ref, qkv_hbm, o_ref, kbuf, vbuf, sem, *, kcol, vcol):
    p, i = pl.program_id(0), pl.program_id(1)
    tq = q_ref.shape[0]
    q = q_ref[...]
    first_head = _head_select((tq, LANES))
    qs = [jnp.where(first_head, q, jnp.zeros_like(q)), jnp.where(first_head, jnp.zeros_like(q), q)]
    causal = lax.broadcasted_iota(I32, (tq, tq), 1) < lax.broadcasted_iota(I32, (tq, tq), 0)
    u = _suffix_ones(tq)
    kd, vd, kp, vp = kd_ref[...], vd_ref[...], kp_ref[...], vp_ref[...]
    cs, accs = [], []
    for qa in qs:
        pv_d, c = _sb_block(qa, kd, vd, jnp.zeros((tq, 1), F32), u, causal=causal)
        pv_p, c = _sb_block(qa, kp, vp, c, u, valid=i > 0)
        cs.append(c)
        accs.append(pv_d + pv_p)

    def fetch(b):
        rows = pl.ds(pl.multiple_of(b * tq, tq), tq)
        _fetch_blocks([
            pltpu.make_async_copy(qkv_hbm.at[rows, pl.ds(pl.multiple_of((kcol + p) * LANES, LANES), LANES)],
                                  kbuf, sem.at[0]),
            pltpu.make_async_copy(qkv_hbm.at[rows, pl.ds(pl.multiple_of((vcol + p) * LANES, LANES), LANES)],
                                  vbuf, sem.at[1])])
        return kbuf[...], vbuf[...]

    accs = _sb_older_blocks(i - 2, fetch, lambda s, kb, vb, c: _sb_block(qs[s], kb, vb, c, u), cs, accs)
    o_ref[...] = jnp.where(first_head, accs[0], accs[1]).astype(o_ref.dtype)


def _sb_prompt(qkv, tq):
    s = qkv.shape[0]
    npair = SB_HEADS // 2
    kcol, vcol = npair, 2 * npair
    prev = lambda col: pl.BlockSpec((tq, LANES), lambda p, i: (jnp.maximum(i - 1, 0), col + p))
    return pl.pallas_call(
        functools.partial(_sb_prompt_kernel, kcol=kcol, vcol=vcol),
        grid=(npair, s // tq),
        in_specs=[pl.BlockSpec((tq, LANES), lambda p, i: (i, p)),
                  pl.BlockSpec((tq, LANES), lambda p, i: (i, kcol + p)),
                  pl.BlockSpec((tq, LANES), lambda p, i: (i, vcol + p)),
                  prev(kcol), prev(vcol),
                  pl.BlockSpec(memory_space=pl.ANY)],
        out_specs=pl.BlockSpec((tq, LANES), lambda p, i: (i, p)),
        out_shape=jax.ShapeDtypeStruct((s, ATT_W), BF16),
        scratch_shapes=[pltpu.VMEM((tq, LANES), BF16), pltpu.VMEM((tq, LANES), BF16),
                        pltpu.SemaphoreType.DMA((2,))],
        compiler_params=_cparams(("arbitrary", "arbitrary")),
        name="sb_prompt",
    )(qkv, qkv, qkv, qkv, qkv, qkv)


def _sb_sample_kernel(q_ref, kn_ref, vn_ref, kp_ref, vp_ref, ck_hbm, cv_hbm, o_ref, kbuf, vbuf, sem, *, layer):
    batch = pl.program_id(0)
    lq = q_ref.shape[0]
    bk = kp_ref.shape[0]
    q, kn, vn = q_ref[...], kn_ref[...], vn_ref[...]
    causal = lax.broadcasted_iota(I32, (lq, lq), 1) < lax.broadcasted_iota(I32, (lq, lq), 0)
    u_new, u_old = _suffix_ones(lq), _suffix_ones(bk)
    head = lambda a, h: a[:, h * HEAD_DIM:(h + 1) * HEAD_DIM]
    qs = [head(q, h) for h in range(SB_HEADS)]
    cs, accs = [], []
    for h in range(SB_HEADS):
        pv_n, c = _sb_block(qs[h], head(kn, h), head(vn, h), jnp.zeros((lq, 1), F32), u_new, causal=causal)
        pv_p, c = _sb_block(qs[h], kp_ref[:, h, :].astype(BF16), vp_ref[:, h, :].astype(BF16), c, u_old)
        cs.append(c)
        accs.append(pv_n + pv_p)

    def fetch(b):
        rows = pl.ds(pl.multiple_of(b * bk, bk), bk)
        _fetch_blocks([pltpu.make_async_copy(ck_hbm.at[layer, batch, rows], kbuf, sem.at[0]),
                       pltpu.make_async_copy(cv_hbm.at[layer, batch, rows], vbuf, sem.at[1])])
        return kbuf, vbuf

    def step(h, kb, vb, c):
        return _sb_block(qs[h], kb[:, h, :].astype(BF16), vb[:, h, :].astype(BF16), c, u_old)

    n_old = ck_hbm.shape[2] // bk
    accs = _sb_older_blocks(n_old - 2, fetch, step, cs, accs)
    o_ref[...] = jnp.concatenate(accs, axis=1).astype(o_ref.dtype)


def _sb_sample(qkv, cache_k, cache_v, layer, lq, bk):
    _, nb, past, nh, hd = cache_k.shape
    assert past % bk == 0
    last = past // bk - 1
    newest = pl.BlockSpec((None, None, bk, nh, hd), lambda b: (layer, b, last, 0, 0))
    return pl.pallas_call(
        functools.partial(_sb_sample_kernel, layer=layer),
        grid=(nb,),
        in_specs=[pl.BlockSpec((lq, ATT_W), lambda b: (b, 0)),
                  pl.BlockSpec((lq, ATT_W), lambda b: (b, 1)),
                  pl.BlockSpec((lq, ATT_W), lambda b: (b, 2)),
                  newest, newest,
                  pl.BlockSpec(memory_space=pl.ANY), pl.BlockSpec(memory_space=pl.ANY)],
        out_specs=pl.BlockSpec((lq, ATT_W), lambda b: (b, 0)),
        out_shape=jax.ShapeDtypeStruct((nb * lq, ATT_W), BF16),
        scratch_shapes=[pltpu.VMEM((bk, nh, hd), cache_k.dtype), pltpu.VMEM((bk, nh, hd), cache_v.dtype),
                        pltpu.SemaphoreType.DMA((2,))],
        compiler_params=_cparams(("arbitrary",)),
        name="sb_sample",
    )(qkv, qkv, qkv, cache_k, cache_v, cache_k, cache_v)


def _softmax_pv(s_blocks, v_blocks):
    m = functools.reduce(jnp.maximum, [jnp.max(s, axis=1, keepdims=True) for s in s_blocks])
    p_blocks = [jnp.exp(s - m) for s in s_blocks]
    denom = functools.reduce(jnp.add, [jnp.sum(p, axis=1, keepdims=True) for p in p_blocks])
    pv = functools.reduce(jnp.add, [_dot(p.astype(BF16), v) for p, v in zip(p_blocks, v_blocks)])
    return pv / denom


def _cb_kernel(q_ref, *refs, sizes, lead_blocks):
    nblk = len(sizes)
    k_refs, v_refs = refs[:nblk], refs[nblk:2 * nblk]
    bias_ref, o_ref = refs[2 * nblk], refs[2 * nblk + 1]
    tq = q_ref.shape[0]
    q = q_ref[...]
    first_head = _head_select((tq, LANES))
    ks = [r[...].astype(BF16) for r in k_refs]
    vs = [r[...].astype(BF16) for r in v_refs]
    offs = np.concatenate([[0], np.cumsum(sizes)])
    outs = []
    for head in range(2):
        qa = jnp.where(first_head if head == 0 else ~first_head, q, jnp.zeros_like(q))
        s_blocks = []
        for n in range(nblk):
            b = bias_ref[head, :, int(offs[n]):int(offs[n + 1])]
            s = _dot_nt(qa, ks[n]) + b
            if n < lead_blocks:
                s = jnp.where(pl.program_id(1) >= lead_blocks - n, s, NEG_INF)
            s_blocks.append(s)
        outs.append(_softmax_pv(s_blocks, vs))
    o_ref[...] = jnp.where(first_head, outs[0], outs[1]).astype(o_ref.dtype)


def _rel_bias(rel_table, lq, lk, q_off):
    m = lq + lk
    pos = jnp.arange(m, dtype=I32)
    j_minus_i = jnp.where(pos < lk, pos, pos - m)
    diag = rel_table[:, jnp.clip(q_off - j_minus_i, -REL_CLIP, REL_CLIP) + REL_CLIP].astype(F32)
    h = rel_table.shape[0]
    tiled = jnp.broadcast_to(diag[:, None, :], (h, lq, m)).reshape(h, lq * m)[:, :lq * (m - 1)]
    return tiled.reshape(h, lq, m - 1)[:, :, :lk]


def _cb_prompt(qkv, rel_table, tq):
    s = qkv.shape[0]
    npair = CB_HEADS // 2
    lead = (CB_LEFT_CHUNKS * CHUNK) // tq
    nblk = lead + 1
    q_pos = lead * tq + jnp.arange(tq, dtype=I32)
    k_pos = jnp.arange(nblk * tq, dtype=I32)
    qc, kc = q_pos // CHUNK, k_pos // CHUNK
    band = (kc[None, :] >= qc[:, None] - CB_LEFT_CHUNKS) & (kc[None, :] <= qc[:, None])
    bias = jnp.where(band[None], _rel_bias(rel_table, tq, nblk * tq, lead * tq), NEG_INF)
    base = 3 * npair
    kv_specs = [pl.BlockSpec((tq, LANES), functools.partial(
        lambda p, i, back, col: (jnp.maximum(i - back, 0), col + p), back=lead - n, col=base + off))
        for off in (npair, 2 * npair) for n in range(nblk)]
    return pl.pallas_call(
        functools.partial(_cb_kernel, sizes=(tq,) * nblk, lead_blocks=lead),
        grid=(npair, s // tq),
        in_specs=[pl.BlockSpec((tq, LANES), lambda p, i: (i, base + p))] + kv_specs
        + [pl.BlockSpec((2, tq, nblk * tq), lambda p, i: (p, 0, 0))],
        out_specs=pl.BlockSpec((tq, LANES), lambda p, i: (i, p)),
        out_shape=jax.ShapeDtypeStruct((s, ATT_W), BF16),
        compiler_params=_cparams(("parallel", "arbitrary")),
        name="cb_prompt",
    )(qkv, *([qkv] * (2 * nblk)), bias)


def _cb_sample(qkv, cache_k, cache_v, rel_table, nb, lq):
    cb_past = cache_k.shape[1]
    npair = CB_HEADS // 2
    bias = _rel_bias(rel_table, lq, cb_past + lq, cb_past)
    base = 3 * npair
    return pl.pallas_call(
        functools.partial(_cb_kernel, sizes=(cb_past, lq), lead_blocks=0),
        grid=(nb, npair),
        in_specs=[pl.BlockSpec((lq, LANES), lambda b, p: (b, base + p)),
                  pl.BlockSpec((None, cb_past, LANES), lambda b, p: (b, 0, p)),
                  pl.BlockSpec((lq, LANES), lambda b, p: (b, base + npair + p)),
                  pl.BlockSpec((None, cb_past, LANES), lambda b, p: (b, 0, p)),
                  pl.BlockSpec((lq, LANES), lambda b, p: (b, base + 2 * npair + p)),
                  pl.BlockSpec((2, lq, cb_past + lq), lambda b, p: (p, 0, 0))],
        out_specs=pl.BlockSpec((lq, LANES), lambda b, p: (b, p)),
        out_shape=jax.ShapeDtypeStruct((nb * lq, ATT_W), BF16),
        compiler_params=_cparams(("parallel", "arbitrary")),
        name="cb_sample",
    )(qkv, cache_k, qkv, cache_v, qkv, bias)


def _memkv_kernel(mem_ref, g_ref, w_ref, k_ref, v_ref):
    h = _rms(mem_ref[...], g_ref[...]).astype(BF16)
    kv = _dot(h, w_ref[...])
    half = k_ref.shape[1]
    k_ref[...] = kv[:, :half]
    v_ref[...] = kv[:, half:]


def _memkv(mem, g, w_bf16):
    n = mem.shape[0]
    half = w_bf16.shape[1] // 2
    return pl.pallas_call(
        _memkv_kernel,
        out_shape=[jax.ShapeDtypeStruct((n, half), F32)] * 2,
        compiler_params=pltpu.CompilerParams(vmem_limit_bytes=VMEM_LIMIT),
        name="memkv",
    )(mem, g, w_bf16)


def _xattn_kernel(q_ref, k_ref, v_ref, o_ref):
    q = q_ref[...]
    k = k_ref[...].astype(BF16)
    v = v_ref[...].astype(BF16)
    for h in range(MEM_HEADS):
        cols = slice(h * MEM_HEAD_DIM, (h + 1) * MEM_HEAD_DIM)
        s = _dot_nt(q[:, cols], k[:, cols]) * MEM_SCALE
        o_ref[:, cols] = _softmax_pv([s], [v[:, cols]]).astype(o_ref.dtype)


def _xattn(qkv, mem_k, mem_v, tq, per_tile_kv):
    t = qkv.shape[0]
    qcol = (N_QKV_SEG - 1)
    if per_tile_kv:
        kv_spec = pl.BlockSpec((None,) + mem_k.shape[1:], lambda i: (i, 0, 0))
    else:
        kv_spec = pl.BlockSpec(mem_k.shape, lambda i: (0, 0))
    return pl.pallas_call(
        _xattn_kernel,
        grid=(t // tq,),
        in_specs=[pl.BlockSpec((tq, ATT_W), lambda i: (i, qcol)), kv_spec, kv_spec],
        out_specs=pl.BlockSpec((tq, ATT_W), lambda i: (i, 0)),
        out_shape=jax.ShapeDtypeStruct((t, ATT_W), BF16),
        compiler_params=_cparams(("parallel",)),
        name="xattn",
    )(qkv, mem_k, mem_v)


def _merge_kernel(x_ref, osb_ref, ocb_ref, om_ref, g_ref, wsb_ref, wcb_ref, wm_ref, wo_ref,
                  gf_ref, wr_ref, br_ref,
                  y_ref, hf_ref, e_ref, gate_ref, rank_ref, cnt_ref, cnt_scr):
    i = pl.program_id(0)
    tm, d = x_ref.shape

    @pl.when(i == 0)
    def _():
        cnt_scr[...] = jnp.zeros_like(cnt_scr)

    m = None
    for n, (o_ref, w_ref) in enumerate(((osb_ref, wsb_ref), (ocb_ref, wcb_ref), (om_ref, wm_ref))):
        term = g_ref[:, n * d:(n + 1) * d].astype(F32) * _dot(o_ref[...], w_ref[...])
        m = term if m is None else m + term
    y = x_ref[...] + _dot(m.astype(BF16), wo_ref[...])
    y_ref[...] = y
    hf = _rms(y, gf_ref[...])
    hf_ref[...] = hf
    logits = _dot(hf.astype(BF16), wr_ref[...]) + br_ref[...]

    ne = logits.shape[1]
    lane = lax.broadcasted_iota(I32, (tm, ne), 1)
    work = logits
    vals, sels, idxs = [], [], []
    for _ in range(TOP_K):
        mx = jnp.max(work, axis=1, keepdims=True)
        idx = jnp.min(jnp.where(work == mx, lane, ne), axis=1, keepdims=True)
        sel = lane == idx
        vals.append(mx)
        idxs.append(idx)
        sels.append(sel)
        work = jnp.where(sel, -jnp.inf, work)
    ex = [jnp.exp(v - vals[0]) for v in vals]
    denom = functools.reduce(jnp.add, ex)

    onehot = functools.reduce(jnp.add, [s.astype(F32) for s in sels])
    before = (lax.broadcasted_iota(I32, (tm, tm), 1) < lax.broadcasted_iota(I32, (tm, tm), 0)).astype(BF16)
    prior = _dot(before, onehot.astype(BF16)) + cnt_scr[...]
    for k in range(TOP_K):
        e_ref[:, k:k + 1] = idxs[k]
        gate_ref[:, k:k + 1] = ex[k] / denom
        rank_ref[:, k:k + 1] = jnp.sum(jnp.where(sels[k], prior, 0.0), axis=1, keepdims=True).astype(I32)
    cnt_scr[...] += jnp.sum(onehot, axis=0, keepdims=True)
    cnt_ref[...] = cnt_scr[...].astype(I32)


def _merge(x, o_sb, o_cb, o_m, gates, w_sb, w_cb, w_m, w_out, g_ffn, w_router, b_router, tm):
    t, d = x.shape
    ne = w_router.shape[1]
    row = lambda i: (i, 0)
    return pl.pallas_call(
        _merge_kernel,
        grid=(t // tm,),
        in_specs=[pl.BlockSpec((tm, d), row)] + [pl.BlockSpec((tm, ATT_W), row)] * 3
        + [pl.BlockSpec((tm, gates.shape[1]), row)]
        + [_resident(a) for a in (w_sb, w_cb, w_m, w_out, g_ffn, w_router, b_router)],
        out_specs=[pl.BlockSpec((tm, d), row), pl.BlockSpec((tm, d), row)]
        + [pl.BlockSpec((tm, TOP_K), row)] * 3 + [pl.BlockSpec((1, ne), lambda i: (0, 0))],
        out_shape=[jax.ShapeDtypeStruct((t, d), F32), jax.ShapeDtypeStruct((t, d), F32),
                   jax.ShapeDtypeStruct((t, TOP_K), I32), jax.ShapeDtypeStruct((t, TOP_K), F32),
                   jax.ShapeDtypeStruct((t, TOP_K), I32), jax.ShapeDtypeStruct((1, ne), I32)],
        scratch_shapes=[pltpu.VMEM((1, ne), F32)],
        compiler_params=_cparams(("arbitrary",)),
        name="merge_router",
    )(x, o_sb, o_cb, o_m, gates, w_sb, w_cb, w_m, w_out, g_ffn, w_router, b_router)


def _row_copies(make_copy, n_rows):
    def start(r, carry):
        for k in range(TOP_K):
            make_copy(r, k).start(priority=k % 2)
        return carry

    def wait(r, carry):
        for k in range(TOP_K):
            make_copy(r, k).wait()
        return carry

    lax.fori_loop(0, n_rows, start, 0, unroll=ROW_UNROLL)
    lax.fori_loop(0, n_rows, wait, 0, unroll=ROW_UNROLL)


def _dispatch_kernel(dest_ref, hf_ref, xs_in_ref, xs_ref, sem):
    del xs_in_ref
    i = pl.program_id(0)
    tm = hf_ref.shape[0]

    def row_copy(r, k):
        dst = dest_ref[(i * tm + r) * TOP_K + k]
        return pltpu.make_async_copy(hf_ref.at[pl.ds(r, 1)], xs_ref.at[pl.ds(dst, 1)], sem)

    _row_copies(row_copy, tm)


def _dispatch(dest_flat, hf, rows, tm):
    t, d = hf.shape
    xs0 = jnp.zeros((rows, d), hf.dtype)
    return pl.pallas_call(
        _dispatch_kernel,
        grid_spec=pltpu.PrefetchScalarGridSpec(
            num_scalar_prefetch=1,
            grid=(t // tm,),
            in_specs=[pl.BlockSpec((tm, d), lambda i, dest: (i, 0)),
                      pl.BlockSpec(memory_space=pl.ANY)],
            out_specs=pl.BlockSpec(memory_space=pl.ANY),
            scratch_shapes=[pltpu.SemaphoreType.DMA]),
        out_shape=jax.ShapeDtypeStruct((rows, d), hf.dtype),
        input_output_aliases={2: 0},
        compiler_params=_cparams(("arbitrary",)),
        name="moe_dispatch",
    )(dest_flat, hf, xs0)


def _expert_kernel(te_ref, nt_ref, xs_ref, wup_ref, wd_ref, bg_ref, bu_ref, bd_ref, ys_ref,
                   wg_scr, wu_scr, wd_scr):
    t = pl.program_id(0)
    valid = t < nt_ref[0]
    changed = jnp.logical_or(t == 0, te_ref[t] != te_ref[jnp.maximum(t - 1, 0)])

    @pl.when(jnp.logical_and(valid, changed))
    def _():
        grp = 2 * LANES
        src = lax.broadcasted_iota(I32, (grp, grp), 0)
        dst = lax.broadcasted_iota(I32, (grp, grp), 1)
        perm = (src == jnp.where(dst < LANES, 2 * dst, 2 * (dst - LANES) + 1)).astype(BF16)
        for n in range(wup_ref.shape[1] // grp):
            w = _dot(wup_ref[:, n * grp:(n + 1) * grp].astype(BF16), perm).astype(BF16)
            wg_scr[:, n * LANES:(n + 1) * LANES] = w[:, :LANES]
            wu_scr[:, n * LANES:(n + 1) * LANES] = w[:, LANES:]
        wd_scr[...] = wd_ref[...].astype(BF16)

    @pl.when(valid)
    def _():
        x = xs_ref[...].astype(BF16)
        g = jnp.minimum(_dot(x, wg_scr[...]) + bg_ref[...], SWIGLU_LIMIT)
        u = jnp.clip(_dot(x, wu_scr[...]) + bu_ref[...], -SWIGLU_LIMIT, SWIGLU_LIMIT)
        act = (u + 1.0) * (g * jax.nn.sigmoid(SWIGLU_ALPHA * g))
        ys_ref[...] = _dot(act.astype(BF16), wd_scr[...]) + bd_ref[...]

    @pl.when(jnp.logical_not(valid))
    def _():
        ys_ref[...] = jnp.zeros_like(ys_ref)


def _experts(tile_e, n_tiles_used, xs, w_up, w_d, b_g, b_u, b_d, tr):
    rows, d = xs.shape
    f = w_d.shape[1]
    wspec = lambda a: pl.BlockSpec((None,) + a.shape[1:], lambda t, te, nt: (te[t], 0, 0))
    return pl.pallas_call(
        _expert_kernel,
        grid_spec=pltpu.PrefetchScalarGridSpec(
            num_scalar_prefetch=2,
            grid=(rows // tr,),
            in_specs=[pl.BlockSpec((tr, d), lambda t, te, nt: (t, 0))]
            + [wspec(a) for a in (w_up, w_d, b_g, b_u, b_d)],
            out_specs=pl.BlockSpec((tr, d), lambda t, te, nt: (t, 0)),
            scratch_shapes=[pltpu.VMEM((d, f), BF16), pltpu.VMEM((d, f), BF16), pltpu.VMEM((f, d), BF16)]),
        out_shape=jax.ShapeDtypeStruct((rows, d), F32),
        compiler_params=_cparams(("arbitrary",)),
        name="moe_experts",
    )(tile_e, n_tiles_used, xs, w_up, w_d, b_g, b_u, b_d)


def _combine_kernel(dest_ref, y_ref, gate_ref, gfin_ref, ys_ref, o_ref, buf, sem):
    i = pl.program_id(0)
    tm = y_ref.shape[0]

    def row_copy(r, k):
        src = dest_ref[(i * tm + r) * TOP_K + k]
        return pltpu.make_async_copy(ys_ref.at[pl.ds(src, 1)], buf.at[k, pl.ds(r, 1)], sem)

    _row_copies(row_copy, tm)
    moe =functools.reduce(jnp.add, [buf[k] * gate_ref[:, k:k + 1] for k in range(TOP_K)])
    o_ref[...] = _rms(y_ref[...] + moe, gfin_ref[...])


def _combine(dest_flat, y, gate, g_final, ys, tm):
    t, d = y.shape
    return pl.pallas_call(
        _combine_kernel,
        grid_spec=pltpu.PrefetchScalarGridSpec(
            num_scalar_prefetch=1,
            grid=(t // tm,),
            in_specs=[pl.BlockSpec((tm, d), lambda i, dest: (i, 0)),
                      pl.BlockSpec((tm, TOP_K), lambda i, dest: (i, 0)),
                      pl.BlockSpec((1, d), lambda i, dest: (0, 0)),
                      pl.BlockSpec(memory_space=pl.ANY)],
            out_specs=pl.BlockSpec((tm, d), lambda i, dest: (i, 0)),
            scratch_shapes=[pltpu.VMEM((TOP_K, tm, d), F32), pltpu.SemaphoreType.DMA]),
        out_shape=jax.ShapeDtypeStruct((t, d), F32),
        compiler_params=_cparams(("arbitrary",)),
        name="moe_combine",
    )(dest_flat, y, gate, g_final, ys)


def _moe(y, hf, top_e, gate, rank, counts, w_up, w_d, b_g, b_u, b_d, g_final, tm, tr):
    t = y.shape[0]
    ne = counts.shape[1]
    n_tiles = (t * TOP_K + ne * (tr - 1) + tr - 1) // tr
    tiles_per_e = (counts[0] + tr - 1) // tr
    eid = jnp.arange(ne, dtype=I32)
    tile_end = jnp.sum(jnp.where(eid[None, :] <= eid[:, None], tiles_per_e[None, :], 0), axis=1)
    row_start = (tile_end - tiles_per_e) * tr
    dest = (jnp.sum(jnp.where(top_e[..., None] == eid, row_start, 0), axis=-1) + rank).reshape(-1)
    tile_e = jnp.minimum(jnp.sum(tile_end[None, :] <= jnp.arange(n_tiles, dtype=I32)[:, None], axis=1),
                         ne - 1).astype(I32)
    xs = _dispatch(dest, hf, n_tiles * tr, tm)
    ys = _experts(tile_e, tile_end[-1:].astype(I32), xs, w_up, w_d, b_g, b_u, b_d, tr)
    return _combine(dest, y, gate, g_final, ys, tm)


def _tile(n, pref):
    return pref if n % pref == 0 else n


def _group(x, attend, weights, tm):
    (g_attn, w_in, b_gate, w_sb, w_cb, w_m, w_out, g_ffn, w_router, b_router,
     w_up, w_d, b_g, b_u, b_d, g_final) = weights
    qkv, ksb, vsb, kcb, vcb, gates = _inproj(x, g_attn, w_in, b_gate, tm)
    o_sb, o_cb, o_m = attend(qkv)
    y, hf, top_e, gate, rank, counts = _merge(
        x, o_sb, o_cb, o_m, gates, w_sb, w_cb, w_m, w_out, g_ffn, w_router, b_router, tm)
    out = _moe(y, hf, top_e, gate, rank, counts, w_up, w_d, b_g, b_u, b_d, g_final,
               _tile(x.shape[0], 256), 512)
    return out, ksb, vsb, kcb, vcb


def kernel(x_prompt, x_sample, mem_prompt, cache_sb_k, cache_sb_v, cache_cb_k, cache_cb_v, cache_mem_k, cache_mem_v, g_attn, w_in, b_gate, rel_table, g_mem, w_mem_kv, w_br_sb, w_br_cb, w_br_mem, w_out, g_ffn, w_router, b_router, w_up, b_up, w_down, b_down, g_final):
    depth = w_in.shape[0]
    assert depth == 1, "one layer: the per-group outputs below are the final residual streams"
    bp, seq, d = x_prompt.shape
    nb, lq, _ = x_sample.shape
    assert bp == 1
    past = cache_sb_k.shape[2]
    band_keep = min(CB_LEFT_CHUNKS * CHUNK, seq)
    l = 0
    row2 = lambda a: a.reshape(1, -1)
    weights = (row2(g_attn[l]), w_in[l].astype(BF16), row2(b_gate[l]),
               w_br_sb[l].astype(BF16), w_br_cb[l].astype(BF16), w_br_mem[l].astype(BF16),
               w_out[l].astype(BF16), row2(g_ffn[l]), w_router[l].astype(BF16), row2(b_router[l]),
               w_up[l], w_down[l],
               b_up[l][:, None, 0::2], b_up[l][:, None, 1::2], b_down[l][:, None, :], row2(g_final))

    mk, mv = _memkv(mem_prompt[0], row2(g_mem[l]), w_mem_kv[l].astype(BF16))

    def attend_prompt(qkv):
        return (_sb_prompt(qkv, _tile(seq, 256)), _cb_prompt(qkv, rel_table[l], _tile(seq, 256)),
                _xattn(qkv, mk, mv, _tile(seq, 512), per_tile_kv=False))

    def attend_sample(qkv):
        flat = lambda c: c.reshape(c.shape[0], c.shape[1], -1)
        return (_sb_sample(qkv, cache_sb_k, cache_sb_v, l, lq, _tile(past, 256)),
                _cb_sample(qkv, flat(cache_cb_k[l]), flat(cache_cb_v[l]), rel_table[l], nb, lq),
                _xattn(qkv, flat(cache_mem_k[l]), flat(cache_mem_v[l]), lq, per_tile_kv=True))

    yp, ksb_p, vsb_p, kcb_p, vcb_p = _group(x_prompt.reshape(seq, d), attend_prompt, weights, _tile(seq, 512))
    ys, ksb_s, vsb_s, kcb_s, vcb_s = _group(x_sample.reshape(nb * lq, d), attend_sample, weights,
                                            _tile(nb * lq, 512))

    heads = lambda a, b, n, h, hd: a.reshape(1, b, n, h, hd)
    return (yp.reshape(1, seq, d), ys.reshape(nb, lq, d),
            heads(ksb_p, 1, seq, SB_HEADS, HEAD_DIM), heads(vsb_p, 1, seq, SB_HEADS, HEAD_DIM),
            heads(kcb_p[seq - band_keep:], 1, band_keep, CB_HEADS, HEAD_DIM),
            heads(vcb_p[seq - band_keep:], 1, band_keep, CB_HEADS, HEAD_DIM),
            heads(mk, 1, N_MEM, MEM_HEADS, MEM_HEAD_DIM), heads(mv, 1, N_MEM, MEM_HEADS, MEM_HEAD_DIM),
            heads(ksb_s, nb, lq, SB_HEADS, HEAD_DIM), heads(vsb_s, nb, lq, SB_HEADS, HEAD_DIM),
            heads(kcb_s, nb, lq, CB_HEADS, HEAD_DIM), heads(vcb_s, nb, lq, CB_HEADS, HEAD_DIM))
```

```python
import functools

import jax
import jax.numpy as jnp
import numpy as np
from jax import lax
from jax.experimental import pallas as pl
from jax.experimental.pallas import tpu as pltpu

F32 = jnp.float32
BF16 = jnp.bfloat16
I32 = jnp.int32

CHUNK = 64
N_MEM = 256
SB_HEADS = 8
CB_HEADS = 8
HEAD_DIM = 64
MEM_HEADS = 4
MEM_HEAD_DIM = 128
CB_LEFT_CHUNKS = 8
REL_CLIP = 128
ATT_W = 512
N_QKV_SEG = 7
N_EXPERTS = 32
TOP_K = 4
SWIGLU_LIMIT = 7.0
SWIGLU_ALPHA = 1.702
RMS_EPS = 1e-6
NEG_INF = -1e30

LANES = 128
VMEM_LIMIT = 56 * 1024 * 1024
SB_UNDERFLOW = 104.0

EXPERT_ROWS = 256
DISPATCH_ROWS = 256
ROW_UNROLL = 8

QK_SCALE = HEAD_DIM ** -0.5
MEM_SCALE = MEM_HEAD_DIM ** -0.5


def _cparams(sem):
    return pltpu.CompilerParams(dimension_semantics=sem, vmem_limit_bytes=VMEM_LIMIT)


def _rms(x, g):
    return x * lax.rsqrt(jnp.mean(x * x, axis=-1, keepdims=True) + RMS_EPS) * g


def _dot(a, b):
    return jnp.dot(a, b, preferred_element_type=F32)


def _dot_nt(a, b):
    return lax.dot_general(a, b, (((1,), (1,)), ((), ())), preferred_element_type=F32)


def _inproj_kernel(x_ref, g_ref, w_ref, bg_ref, qkv_ref, ksb_ref, vsb_ref, kcb_ref, vcb_ref, gate_ref, h_scr):
    h_scr[...] = _rms(x_ref[...], g_ref[...]).astype(BF16)
    f32_out = {1: ksb_ref, 2: vsb_ref, 4: kcb_ref, 5: vcb_ref}
    for seg in range(w_ref.shape[1] // ATT_W):
        cols = slice(seg * ATT_W, (seg + 1) * ATT_W)
        acc = _dot(h_scr[...], w_ref[:, cols])
        if seg < N_QKV_SEG:
            qkv_ref[:, cols] = (acc * (QK_SCALE if seg in (0, 3) else 1.0)).astype(BF16)
            if seg in f32_out:
                f32_out[seg][...] = acc
        else:
            gcols = slice((seg - N_QKV_SEG) * ATT_W, (seg - N_QKV_SEG + 1) * ATT_W)
            gate_ref[:, gcols] = jax.nn.sigmoid(acc + bg_ref[:, gcols]).astype(BF16)


def _resident(a):
    return pl.BlockSpec(a.shape, lambda *_: (0,) * a.ndim, pipeline_mode=pl.Buffered(1))


def _inproj(x, g, w_bf16, b_gate, tm):
    t, d = x.shape
    n_gate = w_bf16.shape[1] - N_QKV_SEG * ATT_W
    row = lambda i: (i, 0)
    f32_spec = pl.BlockSpec((tm, ATT_W), row)
    return pl.pallas_call(
        _inproj_kernel,
        grid=(t // tm,),
        in_specs=[pl.BlockSpec((tm, d), row), _resident(g), _resident(w_bf16), _resident(b_gate)],
        out_specs=[pl.BlockSpec((tm, N_QKV_SEG * ATT_W), row), f32_spec, f32_spec, f32_spec, f32_spec,
                   pl.BlockSpec((tm, n_gate), row)],
        out_shape=[jax.ShapeDtypeStruct((t, N_QKV_SEG * ATT_W), BF16)]
        + [jax.ShapeDtypeStruct((t, ATT_W), F32)] * 4
        + [jax.ShapeDtypeStruct((t, n_gate), BF16)],
        scratch_shapes=[pltpu.VMEM((tm, d), BF16)],
        compiler_params=_cparams(("parallel",)),
        name="inproj",
    )(x, g, w_bf16, b_gate)


def _head(a, h):
    return a[:, h * HEAD_DIM:(h + 1) * HEAD_DIM]


def _head_select(shape):
    return lax.broadcasted_iota(I32, shape, 1) < HEAD_DIM


def _suffix_ones(n):
    return (lax.broadcasted_iota(I32, (n, n), 0) > lax.broadcasted_iota(I32, (n, n), 1)).astype(BF16)


def _sb_block(q, kb, vb, c, u, causal=None, valid=None, seq_minor=False):
    z = _dot(q, kb) if seq_minor else _dot_nt(q, kb)
    sp = jnp.maximum(z, 0.0) + jnp.log1p(jnp.exp(-jnp.abs(z)))
    keep = causal if valid is None else (valid if causal is None else jnp.logical_and(causal, valid))
    l1m = -sp if keep is None else jnp.where(keep, -sp, 0.0)
    hi = l1m.astype(BF16)
    lo = (l1m - hi.astype(F32)).astype(BF16)
    tail = _dot(hi, u) + _dot(lo, u)
    a = jnp.exp((z - sp) + tail + c)
    if keep is not None:
        a = jnp.where(keep, a, 0.0)
    pv = _dot_nt(a.astype(BF16), vb) if seq_minor else _dot(a.astype(BF16), vb)
    return pv, c + jnp.sum(l1m, axis=1, keepdims=True)


def _any_live(cs):
    return (functools.reduce(jnp.maximum, [jnp.max(c) for c in cs]) > -SB_UNDERFLOW).astype(I32)


def _sb_older_blocks(first_block, fetch, step, cs, accs):
    n = len(cs)

    def cond(carry):
        return jnp.logical_and(carry[0] >= 0, carry[1] > 0)

    def body(carry):
        b, cs, accs = carry[0], carry[2:2 + n], carry[2 + n:]
        kb, vb = fetch(b)
        res = [step(s, kb, vb, cs[s]) for s in range(n)]
        cs = [r[1] for r in res]
        return (b - 1, _any_live(cs), *cs, *[acc + r[0] for acc, r in zip(accs, res)])

    out = lax.while_loop(cond, body, (first_block, _any_live(cs), *cs, *accs))
    return out[2 + n:]


def _fetch_blocks(copies):
    for c in copies:
        c.start()
    for c in copies:
        c.wait()


def _sb_prompt_kernel(q_ref, kd_ref, vd_ref, kp_ref, vp_ref, qkv_hbm, o_ref, kbuf, vbuf, sem, *, kcol, vcol):
    p, i = pl.program_id(0), pl.program_id(1)
    tq = q_ref.shape[0]
    q = q_ref[...]
    first_head = _head_select((tq, LANES))
    qs = [jnp.where(first_head, q, jnp.zeros_like(q)), jnp.where(first_head, jnp.zeros_like(q), q)]
    causal = lax.broadcasted_iota(I32, (tq, tq), 1) < lax.broadcasted_iota(I32, (tq, tq), 0)
    u = _suffix_ones(tq)
    kd, vd, kp, vp = kd_ref[...], vd_ref[...], kp_ref[...], vp_ref[...]
    cs, accs = [], []
    for qa in qs:
        pv_d, c = _sb_block(qa, kd, vd, jnp.zeros((tq, 1), F32), u, causal=causal)
        pv_p, c = _sb_block(qa, kp, vp, c, u, valid=i > 0)
        cs.append(c)
        accs.append(pv_d + pv_p)

    def fetch(b):
        rows = pl.ds(pl.multiple_of(b * tq, tq), tq)
        _fetch_blocks([
            pltpu.make_async_copy(qkv_hbm.at[rows, pl.ds(pl.multiple_of((kcol + p) * LANES, LANES), LANES)],
                                  kbuf, sem.at[0]),
            pltpu.make_async_copy(qkv_hbm.at[rows, pl.ds(pl.multiple_of((vcol + p) * LANES, LANES), LANES)],
                                  vbuf, sem.at[1])])
        return kbuf[...], vbuf[...]

    accs = _sb_older_blocks(i - 2, fetch, lambda s, kb, vb, c: _sb_block(qs[s], kb, vb, c, u), cs, accs)
    o_ref[...] = jnp.where(first_head, accs[0], accs[1]).astype(o_ref.dtype)


def _sb_prompt(qkv, tq):
    s = qkv.shape[0]
    npair = SB_HEADS // 2
    kcol, vcol = npair, 2 * npair
    prev = lambda col: pl.BlockSpec((tq, LANES), lambda p, i: (jnp.maximum(i - 1, 0), col + p))
    return pl.pallas_call(
        functools.partial(_sb_prompt_kernel, kcol=kcol, vcol=vcol),
        grid=(npair, s // tq),
        in_specs=[pl.BlockSpec((tq, LANES), lambda p, i: (i, p)),
                  pl.BlockSpec((tq, LANES), lambda p, i: (i, kcol + p)),
                  pl.BlockSpec((tq, LANES), lambda p, i: (i, vcol + p)),
                  prev(kcol), prev(vcol),
                  pl.BlockSpec(memory_space=pl.ANY)],
        out_specs=pl.BlockSpec((tq, LANES), lambda p, i: (i, p)),
        out_shape=jax.ShapeDtypeStruct((s, ATT_W), BF16),
        scratch_shapes=[pltpu.VMEM((tq, LANES), BF16), pltpu.VMEM((tq, LANES), BF16),
                        pltpu.SemaphoreType.DMA((2,))],
        compiler_params=_cparams(("arbitrary", "arbitrary")),
        name="sb_prompt",
    )(qkv, qkv, qkv, qkv, qkv, qkv)


def _sb_sample_kernel(q_ref, kn_ref, vn_ref, kp_ref, vp_ref, ck_hbm, cv_hbm, o_ref, kbuf, vbuf, sem, *, layer):
    batch = pl.program_id(0)
    lq = q_ref.shape[0]
    bk = kp_ref.shape[2]
    q, kn, vn = q_ref[...], kn_ref[...], vn_ref[...]
    causal = lax.broadcasted_iota(I32, (lq, lq), 1) < lax.broadcasted_iota(I32, (lq, lq), 0)
    u_new, u_old = _suffix_ones(lq), _suffix_ones(bk)
    qs = [_head(q, h) for h in range(SB_HEADS)]
    cs, accs = [], []
    for h in range(SB_HEADS):
        pv_n, c = _sb_block(qs[h], _head(kn, h), _head(vn, h), jnp.zeros((lq, 1), F32), u_new, causal=causal)
        pv_p, c = _sb_block(qs[h], kp_ref[h].astype(BF16), vp_ref[h].astype(BF16), c, u_old, seq_minor=True)
        cs.append(c)
        accs.append(pv_n + pv_p)

    def fetch(b):
        cols = pl.ds(pl.multiple_of(b * bk, bk), bk)
        _fetch_blocks([pltpu.make_async_copy(ck_hbm.at[layer, batch, :, :, cols], kbuf, sem.at[0]),
                       pltpu.make_async_copy(cv_hbm.at[layer, batch, :, :, cols], vbuf, sem.at[1])])
        return kbuf, vbuf

    def step(h, kb, vb, c):
        return _sb_block(qs[h], kb[h].astype(BF16), vb[h].astype(BF16), c, u_old, seq_minor=True)

    n_old = ck_hbm.shape[4] // bk
    accs = _sb_older_blocks(n_old - 2, fetch, step, cs, accs)
    o_ref[...] = jnp.concatenate(accs, axis=1).astype(o_ref.dtype)


def _seq_minor(cache):
    return jnp.transpose(cache, (0, 1, 3, 4, 2))


def _sb_sample(qkv, cache_k, cache_v, layer, lq, bk):
    _, nb, nh, hd, past = cache_k.shape
    assert past % bk == 0
    last = past // bk - 1
    newest = pl.BlockSpec((None, None, nh, hd, bk), lambda b: (layer, b, 0, 0, last))
    return pl.pallas_call(
        functools.partial(_sb_sample_kernel, layer=layer),
        grid=(nb,),
        in_specs=[pl.BlockSpec((lq, ATT_W), lambda b: (b, 0)),
                  pl.BlockSpec((lq, ATT_W), lambda b: (b, 1)),
                  pl.BlockSpec((lq, ATT_W), lambda b: (b, 2)),
                  newest, newest,
                  pl.BlockSpec(memory_space=pl.ANY), pl.BlockSpec(memory_space=pl.ANY)],
        out_specs=pl.BlockSpec((lq, ATT_W), lambda b: (b, 0)),
        out_shape=jax.ShapeDtypeStruct((nb * lq, ATT_W), BF16),
        scratch_shapes=[pltpu.VMEM((nh, hd, bk), cache_k.dtype), pltpu.VMEM((nh, hd, bk), cache_v.dtype),
                        pltpu.SemaphoreType.DMA((2,))],
        compiler_params=_cparams(("arbitrary",)),
        name="sb_sample",
    )(qkv, qkv, qkv, cache_k, cache_v, cache_k, cache_v)


def _softmax_pv(s_blocks, v_blocks):
    m = functools.reduce(jnp.maximum, [jnp.max(s, axis=1, keepdims=True) for s in s_blocks])
    p_blocks = [jnp.exp(s - m) for s in s_blocks]
    denom = functools.reduce(jnp.add, [jnp.sum(p, axis=1, keepdims=True) for p in p_blocks])
    pv = functools.reduce(jnp.add, [_dot(p.astype(BF16), v) for p, v in zip(p_blocks, v_blocks)])
    return pv / denom


def _cb_kernel(q_ref, *refs, sizes, lead_blocks):
    nblk = len(sizes)
    k_refs, v_refs = refs[:nblk], refs[nblk:2 * nblk]
    bias_ref, o_ref = refs[2 * nblk], refs[2 * nblk + 1]
    tq = q_ref.shape[0]
    q = q_ref[...]
    first_head = _head_select((tq, LANES))
    ks = [r[...].astype(BF16) for r in k_refs]
    vs = [r[...].astype(BF16) for r in v_refs]
    offs = np.concatenate([[0], np.cumsum(sizes)])
    outs = []
    for head in range(2):
        qa = jnp.where(first_head if head == 0 else ~first_head, q, jnp.zeros_like(q))
        s_blocks = []
        for n in range(nblk):
            b = bias_ref[head, :, int(offs[n]):int(offs[n + 1])]
            s = _dot_nt(qa, ks[n]) + b
            if n < lead_blocks:
                s = jnp.where(pl.program_id(1) >= lead_blocks - n, s, NEG_INF)
            s_blocks.append(s)
        outs.append(_softmax_pv(s_blocks, vs))
    o_ref[...] = jnp.where(first_head, outs[0], outs[1]).astype(o_ref.dtype)


def _rel_bias(rel_table, lq, lk, q_off):
    m = lq + lk
    pos = jnp.arange(m, dtype=I32)
    j_minus_i = jnp.where(pos < lk, pos, pos - m)
    diag = rel_table[:, jnp.clip(q_off - j_minus_i, -REL_CLIP, REL_CLIP) + REL_CLIP].astype(F32)
    h = rel_table.shape[0]
    tiled = jnp.broadcast_to(diag[:, None, :], (h, lq, m)).reshape(h, lq * m)[:, :lq * (m - 1)]
    return tiled.reshape(h, lq, m - 1)[:, :, :lk]


def _cb_prompt(qkv, rel_table, tq):
    s = qkv.shape[0]
    npair = CB_HEADS // 2
    lead = (CB_LEFT_CHUNKS * CHUNK) // tq
    nblk = lead + 1
    q_pos = lead * tq + jnp.arange(tq, dtype=I32)
    k_pos = jnp.arange(nblk * tq, dtype=I32)
    qc, kc = q_pos // CHUNK, k_pos // CHUNK
    band = (kc[None, :] >= qc[:, None] - CB_LEFT_CHUNKS) & (kc[None, :] <= qc[:, None])
    bias = jnp.where(band[None], _rel_bias(rel_table, tq, nblk * tq, lead * tq), NEG_INF)
    base = 3 * npair
    kv_specs = [pl.BlockSpec((tq, LANES), functools.partial(
        lambda p, i, back, col: (jnp.maximum(i - back, 0), col + p), back=lead - n, col=base + off))
        for off in (npair, 2 * npair) for n in range(nblk)]
    return pl.pallas_call(
        functools.partial(_cb_kernel, sizes=(tq,) * nblk, lead_blocks=lead),
        grid=(npair, s // tq),
        in_specs=[pl.BlockSpec((tq, LANES), lambda p, i: (i, base + p))] + kv_specs
        + [pl.BlockSpec((2, tq, nblk * tq), lambda p, i: (p, 0, 0))],
        out_specs=pl.BlockSpec((tq, LANES), lambda p, i: (i, p)),
        out_shape=jax.ShapeDtypeStruct((s, ATT_W), BF16),
        compiler_params=_cparams(("parallel", "arbitrary")),
        name="cb_prompt",
    )(qkv, *([qkv] * (2 * nblk)), bias)


def _cb_sample_kernel(q_ref, kn_ref, vn_ref, ck_ref, cv_ref, bias_ref, o_ref):
    q, kn, vn = q_ref[...], kn_ref[...], vn_ref[...]
    cb_past = ck_ref.shape[2]
    outs = []
    for h in range(CB_HEADS):
        qh = _head(q, h)
        s_old = _dot(qh, ck_ref[h].astype(BF16)) + bias_ref[h, :, :cb_past]
        s_new = _dot_nt(qh, _head(kn, h)) + bias_ref[h, :, cb_past:]
        m = jnp.maximum(jnp.max(s_old, axis=1, keepdims=True), jnp.max(s_new, axis=1, keepdims=True))
        p_old, p_new = jnp.exp(s_old - m), jnp.exp(s_new - m)
        denom = jnp.sum(p_old, axis=1, keepdims=True) + jnp.sum(p_new, axis=1, keepdims=True)
        pv = _dot_nt(p_old.astype(BF16), cv_ref[h].astype(BF16)) + _dot(p_new.astype(BF16), _head(vn, h))
        outs.append(pv / denom)
    o_ref[...] = jnp.concatenate(outs, axis=1).astype(o_ref.dtype)


def _cb_sample(qkv, cache_k, cache_v, rel_table, layer, lq):
    _, nb, nh, hd, cb_past = cache_k.shape
    bias = _rel_bias(rel_table, lq, cb_past + lq, cb_past)
    cache_spec = pl.BlockSpec((None, None, nh, hd, cb_past), lambda b: (layer, b, 0, 0, 0))
    qcol = 3
    return pl.pallas_call(
        _cb_sample_kernel,
        grid=(nb,),
        in_specs=[pl.BlockSpec((lq, ATT_W), lambda b: (b, qcol)),
                  pl.BlockSpec((lq, ATT_W), lambda b: (b, qcol + 1)),
                  pl.BlockSpec((lq, ATT_W), lambda b: (b, qcol + 2)),
                  cache_spec, cache_spec, _resident(bias)],
        out_specs=pl.BlockSpec((lq, ATT_W), lambda b: (b, 0)),
        out_shape=jax.ShapeDtypeStruct((nb * lq, ATT_W), BF16),
        compiler_params=_cparams(("parallel",)),
        name="cb_sample",
    )(qkv, qkv, qkv, cache_k, cache_v, bias)


def _memkv_kernel(mem_ref, g_ref, w_ref, k_ref, v_ref):
    h = _rms(mem_ref[...], g_ref[...]).astype(BF16)
    kv = _dot(h, w_ref[...])
    half = k_ref.shape[1]
    k_ref[...] = kv[:, :half]
    v_ref[...] = kv[:, half:]


def _memkv(mem, g, w_bf16):
    n = mem.shape[0]
    half = w_bf16.shape[1] // 2
    return pl.pallas_call(
        _memkv_kernel,
        out_shape=[jax.ShapeDtypeStruct((n, half), F32)] * 2,
        compiler_params=pltpu.CompilerParams(vmem_limit_bytes=VMEM_LIMIT),
        name="memkv",
    )(mem, g, w_bf16)


def _xattn_kernel(q_ref, k_ref, v_ref, o_ref):
    q = q_ref[...]
    k = k_ref[...].astype(BF16)
    v = v_ref[...].astype(BF16)
    for h in range(MEM_HEADS):
        cols = slice(h * MEM_HEAD_DIM, (h + 1) * MEM_HEAD_DIM)
        s = _dot_nt(q[:, cols], k[:, cols]) * MEM_SCALE
        o_ref[:, cols] = _softmax_pv([s], [v[:, cols]]).astype(o_ref.dtype)


def _xattn(qkv, mem_k, mem_v, tq, per_tile_kv):
    t = qkv.shape[0]
    qcol = (N_QKV_SEG - 1)
    if per_tile_kv:
        kv_spec = pl.BlockSpec((None,) + mem_k.shape[1:], lambda i: (i, 0, 0))
    else:
        kv_spec = pl.BlockSpec(mem_k.shape, lambda i: (0, 0))
    return pl.pallas_call(
        _xattn_kernel,
        grid=(t // tq,),
        in_specs=[pl.BlockSpec((tq, ATT_W), lambda i: (i, qcol)), kv_spec, kv_spec],
        out_specs=pl.BlockSpec((tq, ATT_W), lambda i: (i, 0)),
        out_shape=jax.ShapeDtypeStruct((t, ATT_W), BF16),
        compiler_params=_cparams(("parallel",)),
        name="xattn",
    )(qkv, mem_k, mem_v)


def _merge_kernel(x_ref, osb_ref, ocb_ref, om_ref, g_ref, wsb_ref, wcb_ref, wm_ref, wo_ref,
                  gf_ref, wr_ref, br_ref, cnt0_ref,
                  y_ref, hf_ref, e_ref, gate_ref, rank_ref, cnt_ref, cnt_scr):
    i = pl.program_id(0)
    tm, d = x_ref.shape

    @pl.when(i == 0)
    def _():
        cnt_scr[...] = cnt0_ref[...].astype(F32)

    m = None
    for n, (o_ref, w_ref) in enumerate(((osb_ref, wsb_ref), (ocb_ref, wcb_ref), (om_ref, wm_ref))):
        term = g_ref[:, n * d:(n + 1) * d].astype(F32) * _dot(o_ref[...], w_ref[...])
        m = term if m is None else m + term
    y = x_ref[...] + _dot(m.astype(BF16), wo_ref[...])
    y_ref[...] = y
    hf = _rms(y, gf_ref[...])
    hf_ref[...] = hf
    logits = _dot(hf.astype(BF16), wr_ref[...]) + br_ref[...]

    ne = logits.shape[1]
    lane = lax.broadcasted_iota(I32, (tm, ne), 1)
    work = logits
    vals, sels, idxs = [], [], []
    for _ in range(TOP_K):
        mx = jnp.max(work, axis=1, keepdims=True)
        idx = jnp.min(jnp.where(work == mx, lane, ne), axis=1, keepdims=True)
        sel = lane == idx
        vals.append(mx)
        idxs.append(idx)
        sels.append(sel)
        work = jnp.where(sel, -jnp.inf, work)
    ex = [jnp.exp(v - vals[0]) for v in vals]
    denom = functools.reduce(jnp.add, ex)

    onehot = functools.reduce(jnp.add, [s.astype(F32) for s in sels])
    before = (lax.broadcasted_iota(I32, (tm, tm), 1) < lax.broadcasted_iota(I32, (tm, tm), 0)).astype(BF16)
    prior = _dot(before, onehot.astype(BF16)) + cnt_scr[...]
    for k in range(TOP_K):
        e_ref[:, k:k + 1] = idxs[k]
        gate_ref[:, k:k + 1] = ex[k] / denom
        rank_ref[:, k:k + 1] = jnp.sum(jnp.where(sels[k], prior, 0.0), axis=1, keepdims=True).astype(I32)
    cnt_scr[...] += jnp.sum(onehot, axis=0, keepdims=True)
    cnt_ref[...] = cnt_scr[...].astype(I32)


def _merge(x, o_sb, o_cb, o_m, gates, w_sb, w_cb, w_m, w_out, g_ffn, w_router, b_router, cnt0, tm):
    t, d = x.shape
    ne = w_router.shape[1]
    row = lambda i: (i, 0)
    return pl.pallas_call(
        _merge_kernel,
        grid=(t // tm,),
        in_specs=[pl.BlockSpec((tm, d), row)] + [pl.BlockSpec((tm, ATT_W), row)] * 3
        + [pl.BlockSpec((tm, gates.shape[1]), row)]
        + [_resident(a) for a in (w_sb, w_cb, w_m, w_out, g_ffn, w_router, b_router, cnt0)],
        out_specs=[pl.BlockSpec((tm, d), row), pl.BlockSpec((tm, d), row)]
        + [pl.BlockSpec((tm, TOP_K), row)] * 3 + [pl.BlockSpec((1, ne), lambda i: (0, 0))],
        out_shape=[jax.ShapeDtypeStruct((t, d), F32), jax.ShapeDtypeStruct((t, d), F32),
                   jax.ShapeDtypeStruct((t, TOP_K), I32), jax.ShapeDtypeStruct((t, TOP_K), F32),
                   jax.ShapeDtypeStruct((t, TOP_K), I32), jax.ShapeDtypeStruct((1, ne), I32)],
        scratch_shapes=[pltpu.VMEM((1, ne), F32)],
        compiler_params=_cparams(("arbitrary",)),
        name="merge_router",
    )(x, o_sb, o_cb, o_m, gates, w_sb, w_cb, w_m, w_out, g_ffn, w_router, b_router, cnt0)


def _row_copies(make_copy, n_rows):
    def start(r, carry):
        for k in range(TOP_K):
            make_copy(r, k).start(priority=k % 2)
        return carry

    def wait(r, carry):
        for k in range(TOP_K):
            make_copy(r, k).wait()
        return carry

    lax.fori_loop(0, n_rows, start, 0, unroll=ROW_UNROLL)
    lax.fori_loop(0, n_rows, wait, 0, unroll=ROW_UNROLL)


def _n_chunks(count):
    return lax.shift_right_logical(count + (EXPERT_ROWS - 1), EXPERT_ROWS.bit_length() - 1)


def _used_rows(start_ref, cnt_ref):
    last = cnt_ref.shape[0] - 1
    return start_ref[last] + _n_chunks(cnt_ref[last]) * EXPERT_ROWS


def _zero_chunks(zeros_ref, dst_hbm, sem, n, chunk_row):
    def copy(c):
        rows = pl.ds(pl.multiple_of(chunk_row(c), EXPERT_ROWS), EXPERT_ROWS)
        return pltpu.make_async_copy(zeros_ref, dst_hbm.at[rows], sem)

    def start(c, carry):
        copy(c).start()
        return carry

    def wait(c, carry):
        copy(c).wait()
        return carry

    lax.fori_loop(0, n, start, 0)
    lax.fori_loop(0, n, wait, 0)


def _zero_tail(zeros_ref, dst_hbm, sem, start_ref, cnt_ref):
    used = _used_rows(start_ref, cnt_ref)
    n_tail = lax.shift_right_logical(dst_hbm.shape[0] - used, EXPERT_ROWS.bit_length() - 1)
    _zero_chunks(zeros_ref, dst_hbm, sem, n_tail, lambda c: used + c * EXPERT_ROWS)


def _dispatch_kernel(dest_ref, start_ref, cnt_ref, *refs, tiles):
    hf_refs, (xs_ref, sem, zbuf, zsem) = refs[:len(tiles)], refs[len(tiles):]
    i = pl.program_id(0)
    tm = hf_refs[0].shape[0]

    @pl.when(i == 0)
    def _():
        zbuf[...] = jnp.zeros_like(zbuf)
        last_row = xs_ref.shape[0] - EXPERT_ROWS

        def last_chunk(e):
            return jnp.minimum(start_ref[e] + jnp.maximum(_n_chunks(cnt_ref[e]) - 1, 0) * EXPERT_ROWS, last_row)

        _zero_chunks(zbuf, xs_ref, zsem, cnt_ref.shape[0], last_chunk)
        _zero_tail(zbuf, xs_ref, zsem, start_ref, cnt_ref)

    first_tile = 0
    for hf_ref, n_tiles in zip(hf_refs, tiles):
        @pl.when(jnp.logical_and(i >= first_tile, i < first_tile + n_tiles))
        def _(hf_ref=hf_ref):
            def row_copy(r, k):
                dst = dest_ref[(i * tm + r) * TOP_K + k]
                return pltpu.make_async_copy(hf_ref.at[pl.ds(r, 1)], xs_ref.at[pl.ds(dst, 1)], sem)

            _row_copies(row_copy, tm)
        first_tile += n_tiles


def _dispatch(dest_flat, row_start, counts, hfs, rows, tm):
    d = hfs[0].shape[1]
    tiles = tuple(hf.shape[0] // tm for hf in hfs)
    firsts = np.concatenate([[0], np.cumsum(tiles)[:-1]])
    specs = [pl.BlockSpec((tm, d), functools.partial(
        lambda i, *_, first, n: (jnp.clip(i - first, 0, n - 1), 0), first=int(f), n=n))
        for f, n in zip(firsts, tiles)]
    return pl.pallas_call(
        functools.partial(_dispatch_kernel, tiles=tiles),
        grid_spec=pltpu.PrefetchScalarGridSpec(
            num_scalar_prefetch=3,
            grid=(sum(tiles),),
            in_specs=specs,
            out_specs=pl.BlockSpec(memory_space=pl.ANY),
            scratch_shapes=[pltpu.SemaphoreType.DMA, pltpu.VMEM((EXPERT_ROWS, d), hfs[0].dtype),
                            pltpu.SemaphoreType.DMA]),
        out_shape=jax.ShapeDtypeStruct((rows, d), hfs[0].dtype),
        compiler_params=_cparams(("arbitrary",)),
        name="moe_dispatch",
    )(dest_flat, row_start, counts, *hfs)


def _expert_kernel(start_ref, cnt_ref, xs_hbm, wup_ref, wd_ref, bg_ref, bu_ref, bd_ref, ys_hbm,
                   wg_scr, wu_scr, wd_scr, xbuf, ybuf, sem_in, sem_out):
    e = pl.program_id(0)
    row0 = start_ref[e]
    n = _n_chunks(cnt_ref[e])

    def rows(j):
        return pl.ds(pl.multiple_of(row0 + j * EXPERT_ROWS, EXPERT_ROWS), EXPERT_ROWS)

    def in_copy(j, slot):
        return pltpu.make_async_copy(xs_hbm.at[rows(j)], xbuf.at[slot], sem_in.at[slot])

    def out_copy(j, slot):
        return pltpu.make_async_copy(ybuf.at[slot], ys_hbm.at[rows(j)], sem_out.at[slot])

    @pl.when(n > 0)
    def _():
        in_copy(0, 0).start()
        grp = 2 * LANES
        src = lax.broadcasted_iota(I32, (grp, grp), 0)
        dst = lax.broadcasted_iota(I32, (grp, grp), 1)
        perm = (src == jnp.where(dst < LANES, 2 * dst, 2 * (dst - LANES) + 1)).astype(BF16)
        for c in range(wup_ref.shape[1] // grp):
            w = _dot(wup_ref[:, c * grp:(c + 1) * grp].astype(BF16), perm).astype(BF16)
            wg_scr[:, c * LANES:(c + 1) * LANES] = w[:, :LANES]
            wu_scr[:, c * LANES:(c + 1) * LANES] = w[:, LANES:]
        wd_scr[...] = wd_ref[...].astype(BF16)

        def chunk(j, carry):
            slot = lax.rem(j, 2)
            in_copy(j, slot).wait()

            @pl.when(j + 1 < n)
            def _():
                in_copy(j + 1, 1 - slot).start()

            @pl.when(j >= 2)
            def _():
                out_copy(j - 2, slot).wait()

            x = xbuf[slot].astype(BF16)
            g = jnp.minimum(_dot(x, wg_scr[...]) + bg_ref[...], SWIGLU_LIMIT)
            u = jnp.clip(_dot(x, wu_scr[...]) + bu_ref[...], -SWIGLU_LIMIT, SWIGLU_LIMIT)
            act = (u + 1.0) * (g * jax.nn.sigmoid(SWIGLU_ALPHA * g))
            ybuf[slot] = _dot(act.astype(BF16), wd_scr[...]) + bd_ref[...]
            out_copy(j, slot).start()
            return carry

        lax.fori_loop(0, n, chunk, 0)

        @pl.when(n >= 2)
        def _():
            out_copy(n - 2, lax.rem(n, 2)).wait()

        out_copy(n - 1, lax.rem(n - 1, 2)).wait()

    @pl.when(e == pl.num_programs(0) - 1)
    def _():
        ybuf[0] = jnp.zeros(ybuf.shape[1:], ybuf.dtype)
        _zero_tail(ybuf.at[0], ys_hbm, sem_out.at[0], start_ref, cnt_ref)


def _experts(row_start, counts, xs, w_up, w_d, b_g, b_u, b_d):
    rows, d = xs.shape
    ne, f = w_d.shape[0], w_d.shape[1]
    wspec = lambda a: pl.BlockSpec((None,) + a.shape[1:], lambda e, *_: (e, 0, 0))
    hbm = pl.BlockSpec(memory_space=pl.ANY)
    return pl.pallas_call(
        _expert_kernel,
        grid_spec=pltpu.PrefetchScalarGridSpec(
            num_scalar_prefetch=2,
            grid=(ne,),
            in_specs=[hbm] + [wspec(a) for a in (w_up, w_d, b_g, b_u, b_d)],
            out_specs=hbm,
            scratch_shapes=[pltpu.VMEM((d, f), BF16), pltpu.VMEM((d, f), BF16), pltpu.VMEM((f, d), BF16),
                            pltpu.VMEM((2, EXPERT_ROWS, d), xs.dtype), pltpu.VMEM((2, EXPERT_ROWS, d), F32),
                            pltpu.SemaphoreType.DMA((2,)), pltpu.SemaphoreType.DMA((2,))]),
        out_shape=jax.ShapeDtypeStruct((rows, d), F32),
        compiler_params=_cparams(("arbitrary",)),
        name="moe_experts",
    )(row_start, counts, xs, w_up, w_d, b_g, b_u, b_d)


def _combine_kernel(dest_ref, y_ref, gate_ref, gfin_ref, ys_ref, o_ref, buf, sem):
    i = pl.program_id(0)
    tm = y_ref.shape[0]

    def row_copy(r, k):
        src = dest_ref[(i * tm + r) * TOP_K + k]
        return pltpu.make_async_copy(ys_ref.at[pl.ds(src, 1)], buf.at[k, pl.ds(r, 1)], sem)

    _row_copies(row_copy, tm)
    moe =functools.reduce(jnp.add, [buf[k] * gate_ref[:, k:k + 1] for k in range(TOP_K)])
    o_ref[...] = _rms(y_ref[...] + moe, gfin_ref[...])


def _combine(dest_flat, y, gate, g_final, ys, tm):
    t, d = y.shape
    return pl.pallas_call(
        _combine_kernel,
        grid_spec=pltpu.PrefetchScalarGridSpec(
            num_scalar_prefetch=1,
            grid=(t // tm,),
            in_specs=[pl.BlockSpec((tm, d), lambda i, dest: (i, 0)),
                      pl.BlockSpec((tm, TOP_K), lambda i, dest: (i, 0)),
                      pl.BlockSpec((1, d), lambda i, dest: (0, 0)),
                      pl.BlockSpec(memory_space=pl.ANY)],
            out_specs=pl.BlockSpec((tm, d), lambda i, dest: (i, 0)),
            scratch_shapes=[pltpu.VMEM((TOP_K, tm, d), F32), pltpu.SemaphoreType.DMA]),
        out_shape=jax.ShapeDtypeStruct((t, d), F32),
        compiler_params=_cparams(("arbitrary",)),
        name="moe_combine",
    )(dest_flat, y, gate, g_final, ys)


def _moe(groups, counts, w_up, w_d, b_g, b_u, b_d, g_final):
    ne = counts.shape[1]
    n_pairs = sum(g[0].shape[0] for g in groups) * TOP_K
    rows = (n_pairs + ne * (EXPERT_ROWS - 1) + EXPERT_ROWS - 1) // EXPERT_ROWS * EXPERT_ROWS
    counts = counts[0]
    chunks = (counts + EXPERT_ROWS - 1) // EXPERT_ROWS
    eid = jnp.arange(ne, dtype=I32)
    row_start = jnp.sum(jnp.where(eid[None, :] < eid[:, None], chunks[None, :], 0), axis=1) * EXPERT_ROWS
    dests = [(jnp.sum(jnp.where(top_e[..., None] == eid, row_start, 0), axis=-1) + rank).reshape(-1)
             for _, _, top_e, _, rank in groups]
    xs = _dispatch(jnp.concatenate(dests), row_start, counts, [g[1] for g in groups], rows, DISPATCH_ROWS)
    ys = _experts(row_start, counts, xs, w_up, w_d, b_g, b_u, b_d)
    return [_combine(dest, y, gate, g_final, ys, DISPATCH_ROWS)
            for (y, _, _, gate, _), dest in zip(groups, dests)]


def _tile(n, pref):
    return pref if n % pref == 0 else n


def _attn_block(x, attend, weights, cnt0, tm):
    g_attn, w_in, b_gate, w_sb, w_cb, w_m, w_out, g_ffn, w_router, b_router = weights
    qkv, ksb, vsb, kcb, vcb, gates = _inproj(x, g_attn, w_in, b_gate, tm)
    o_sb, o_cb, o_m = attend(qkv)
    routed = _merge(x, o_sb, o_cb, o_m, gates, w_sb, w_cb, w_m, w_out, g_ffn, w_router, b_router, cnt0, tm)
    return routed, (ksb, vsb, kcb, vcb)


def kernel(x_prompt, x_sample, mem_prompt, cache_sb_k, cache_sb_v, cache_cb_k, cache_cb_v, cache_mem_k, cache_mem_v, g_attn, w_in, b_gate, rel_table, g_mem, w_mem_kv, w_br_sb, w_br_cb, w_br_mem, w_out, g_ffn, w_router, b_router, w_up, b_up, w_down, b_down, g_final):
    depth = w_in.shape[0]
    assert depth == 1, "one layer: the per-group outputs below are the final residual streams"
    bp, seq, d = x_prompt.shape
    nb, lq, _ = x_sample.shape
    assert bp == 1
    past = cache_sb_k.shape[2]
    band_keep = min(CB_LEFT_CHUNKS * CHUNK, seq)
    l = 0
    row2 = lambda a: a.reshape(1, -1)
    weights = (row2(g_attn[l]), w_in[l].astype(BF16), row2(b_gate[l]),
               w_br_sb[l].astype(BF16), w_br_cb[l].astype(BF16), w_br_mem[l].astype(BF16),
               w_out[l].astype(BF16), row2(g_ffn[l]), w_router[l].astype(BF16), row2(b_router[l]))

    mk, mv = _memkv(mem_prompt[0], row2(g_mem[l]), w_mem_kv[l].astype(BF16))

    def attend_prompt(qkv):
        return (_sb_prompt(qkv, _tile(seq, 256)), _cb_prompt(qkv, rel_table[l], _tile(seq, 256)),
                _xattn(qkv, mk, mv, _tile(seq, 512), per_tile_kv=False))

    def attend_sample(qkv):
        flat = lambda c: c.reshape(c.shape[0], c.shape[1], -1)
        return (_sb_sample(qkv, _seq_minor(cache_sb_k), _seq_minor(cache_sb_v), l, lq, _tile(past, 256)),
                _cb_sample(qkv, _seq_minor(cache_cb_k), _seq_minor(cache_cb_v), rel_table[l], l, lq),
                _xattn(qkv, flat(cache_mem_k[l]), flat(cache_mem_v[l]), lq, per_tile_kv=True))

    no_tokens = jnp.zeros((1, w_router.shape[2]), I32)
    routed_p, (ksb_p, vsb_p, kcb_p, vcb_p) = _attn_block(
        x_prompt.reshape(seq, d), attend_prompt, weights, no_tokens, _tile(seq, 512))
    routed_s, (ksb_s, vsb_s, kcb_s, vcb_s) = _attn_block(
        x_sample.reshape(nb * lq, d), attend_sample, weights, routed_p[5], _tile(nb * lq, 512))
    yp, ys = _moe([routed_p[:5], routed_s[:5]], routed_s[5], w_up[l], w_down[l],
                  b_up[l][:, None, 0::2], b_up[l][:, None, 1::2], b_down[l][:, None, :], row2(g_final))

    heads = lambda a, b, n, h, hd: a.reshape(1, b, n, h, hd)
    return (yp.reshape(1, seq, d), ys.reshape(nb, lq, d),
            heads(ksb_p, 1, seq, SB_HEADS, HEAD_DIM), heads(vsb_p, 1, seq, SB_HEADS, HEAD_DIM),
            heads(kcb_p[seq - band_keep:], 1, band_keep, CB_HEADS, HEAD_DIM),
            heads(vcb_p[seq - band_keep:], 1, band_keep, CB_HEADS, HEAD_DIM),
            heads(mk, 1, N_MEM, MEM_HEADS, MEM_HEAD_DIM), heads(mv, 1, N_MEM, MEM_HEADS, MEM_HEAD_DIM),
            heads(ksb_s, nb, lq, SB_HEADS, HEAD_DIM), heads(vsb_s, nb, lq, SB_HEADS, HEAD_DIM),
            heads(kcb_s, nb, lq, CB_HEADS, HEAD_DIM), heads(vcb_s, nb, lq, CB_HEADS, HEAD_DIM))
```

```python
import functools

import jax
import jax.numpy as jnp
import numpy as np
from jax import lax
from jax.experimental import pallas as pl
from jax.experimental.pallas import tpu as pltpu

F32 = jnp.float32
BF16 = jnp.bfloat16
I32 = jnp.int32

CHUNK = 64
N_MEM = 256
SB_HEADS = 8
CB_HEADS = 8
HEAD_DIM = 64
MEM_HEADS = 4
MEM_HEAD_DIM = 128
CB_LEFT_CHUNKS = 8
REL_CLIP = 128
ATT_W = 512
N_QKV_SEG = 7
N_EXPERTS = 32
TOP_K = 4
SWIGLU_LIMIT = 7.0
SWIGLU_ALPHA = 1.702
RMS_EPS = 1e-6
NEG_INF = -1e30

LANES = 128
VMEM_LIMIT = 56 * 1024 * 1024
SB_UNDERFLOW = 104.0
LOG2_E = 1.4426950408889634

EXPERT_ROWS = 512
DISPATCH_ROWS = 1024
COMBINE_ROWS = 512
ROW_UNROLL = 8

QK_SCALE = HEAD_DIM ** -0.5
MEM_SCALE = MEM_HEAD_DIM ** -0.5


def _cparams(sem):
    return pltpu.CompilerParams(dimension_semantics=sem, vmem_limit_bytes=VMEM_LIMIT)


def _rms(x, g):
    return x * lax.rsqrt(jnp.mean(x * x, axis=-1, keepdims=True) + RMS_EPS) * g


def _dot(a, b):
    return jnp.dot(a, b, preferred_element_type=F32)


def _dot_nt(a, b):
    return lax.dot_general(a, b, (((1,), (1,)), ((), ())), preferred_element_type=F32)


def _inproj_kernel(x_ref, g_ref, w_ref, bg_ref, qkv_ref, ksb_ref, vsb_ref, kcb_ref, vcb_ref, gate_ref, h_scr):
    h_scr[...] = _rms(x_ref[...], g_ref[...]).astype(BF16)
    f32_out = {1: ksb_ref, 2: vsb_ref, 4: kcb_ref, 5: vcb_ref}
    for seg in range(w_ref.shape[1] // ATT_W):
        cols = slice(seg * ATT_W, (seg + 1) * ATT_W)
        acc = _dot(h_scr[...], w_ref[:, cols])
        if seg < N_QKV_SEG:
            qkv_ref[:, cols] = (acc * (QK_SCALE if seg in (0, 3) else 1.0)).astype(BF16)
            if seg in f32_out:
                f32_out[seg][...] = acc
        else:
            gcols = slice((seg - N_QKV_SEG) * ATT_W, (seg - N_QKV_SEG + 1) * ATT_W)
            gate_ref[:, gcols] = jax.nn.sigmoid(acc + bg_ref[:, gcols]).astype(BF16)


def _resident(a):
    return pl.BlockSpec(a.shape, lambda *_: (0,) * a.ndim, pipeline_mode=pl.Buffered(1))


def _inproj(x, g, w_bf16, b_gate, tm):
    t, d = x.shape
    n_gate = w_bf16.shape[1] - N_QKV_SEG * ATT_W
    row = lambda i: (i, 0)
    f32_spec = pl.BlockSpec((tm, ATT_W), row)
    return pl.pallas_call(
        _inproj_kernel,
        grid=(t // tm,),
        in_specs=[pl.BlockSpec((tm, d), row), _resident(g), _resident(w_bf16), _resident(b_gate)],
        out_specs=[pl.BlockSpec((tm, N_QKV_SEG * ATT_W), row), f32_spec, f32_spec, f32_spec, f32_spec,
                   pl.BlockSpec((tm, n_gate), row)],
        out_shape=[jax.ShapeDtypeStruct((t, N_QKV_SEG * ATT_W), BF16)]
        + [jax.ShapeDtypeStruct((t, ATT_W), F32)] * 4
        + [jax.ShapeDtypeStruct((t, n_gate), BF16)],
        scratch_shapes=[pltpu.VMEM((tm, d), BF16)],
        compiler_params=_cparams(("parallel",)),
        name="inproj",
    )(x, g, w_bf16, b_gate)


def _head(a, h):
    return a[:, h * HEAD_DIM:(h + 1) * HEAD_DIM]


def _head_select(shape):
    return lax.broadcasted_iota(I32, shape, 1) < HEAD_DIM


def _suffix_ones(n):
    return (lax.broadcasted_iota(I32, (n, n), 0) > lax.broadcasted_iota(I32, (n, n), 1)).astype(BF16)


def _sb_block(q, kb, vb, c, u, causal=None, valid=None, seq_minor=False):
    z2 = (_dot(q, kb) if seq_minor else _dot_nt(q, kb)) * LOG2_E
    sp = jnp.maximum(z2, 0.0) + jnp.log2(1.0 + jnp.exp2(-jnp.abs(z2)))
    keep = causal if valid is None else (valid if causal is None else jnp.logical_and(causal, valid))
    l1m = -sp if keep is None else jnp.where(keep, -sp, 0.0)
    hi = l1m.astype(BF16)
    lo = (l1m - hi.astype(F32)).astype(BF16)
    tail = _dot(hi, u) + _dot(lo, u)
    a = jnp.exp2((z2 - sp) + tail + c)
    if keep is not None:
        a = jnp.where(keep, a, 0.0)
    pv = _dot_nt(a.astype(BF16), vb) if seq_minor else _dot(a.astype(BF16), vb)
    return pv, c + jnp.sum(l1m, axis=1, keepdims=True)


def _any_live(cs):
    return (functools.reduce(jnp.maximum, [jnp.max(c) for c in cs]) > -SB_UNDERFLOW * LOG2_E).astype(I32)


def _sb_older_blocks(first_block, fetch, step, cs, accs):
    n = len(cs)

    def cond(carry):
        return jnp.logical_and(carry[0] >= 0, carry[1] > 0)

    def body(carry):
        b, cs, accs = carry[0], carry[2:2 + n], carry[2 + n:]
        kb, vb = fetch(b)
        res = [step(s, kb, vb, cs[s]) for s in range(n)]
        cs = [r[1] for r in res]
        return (b - 1, _any_live(cs), *cs, *[acc + r[0] for acc, r in zip(accs, res)])

    out = lax.while_loop(cond, body, (first_block, _any_live(cs), *cs, *accs))
    return out[2 + n:]


def _fetch_blocks(copies):
    for c in copies:
        c.start()
    for c in copies:
        c.wait()


def _sb_prompt_kernel(q_ref, kd_ref, vd_ref, kp_ref, vp_ref, qkv_hbm, o_ref, kbuf, vbuf, sem, *, kcol, vcol):
    p, i = pl.program_id(0), pl.program_id(1)
    tq = q_ref.shape[0]
    q = q_ref[...]
    first_head = _head_select((tq, LANES))
    qs = [jnp.where(first_head, q, jnp.zeros_like(q)), jnp.where(first_head, jnp.zeros_like(q), q)]
    causal = lax.broadcasted_iota(I32, (tq, tq), 1) < lax.broadcasted_iota(I32, (tq, tq), 0)
    u = _suffix_ones(tq)
    kd, vd, kp, vp = kd_ref[...], vd_ref[...], kp_ref[...], vp_ref[...]
    cs, accs = [], []
    for qa in qs:
        pv_d, c = _sb_block(qa, kd, vd, jnp.zeros((tq, 1), F32), u, causal=causal)
        pv_p, c = _sb_block(qa, kp, vp, c, u, valid=i > 0)
        cs.append(c)
        accs.append(pv_d + pv_p)

    def fetch(b):
        rows = pl.ds(pl.multiple_of(b * tq, tq), tq)
        _fetch_blocks([
            pltpu.make_async_copy(qkv_hbm.at[rows, pl.ds(pl.multiple_of((kcol + p) * LANES, LANES), LANES)],
                                  kbuf, sem.at[0]),
            pltpu.make_async_copy(qkv_hbm.at[rows, pl.ds(pl.multiple_of((vcol + p) * LANES, LANES), LANES)],
                                  vbuf, sem.at[1])])
        return kbuf[...], vbuf[...]

    accs = _sb_older_blocks(i - 2, fetch, lambda s, kb, vb, c: _sb_block(qs[s], kb, vb, c, u), cs, accs)
    o_ref[...] = jnp.where(first_head, accs[0], accs[1]).astype(o_ref.dtype)


def _sb_prompt(qkv, tq):
    s = qkv.shape[0]
    npair = SB_HEADS // 2
    kcol, vcol = npair, 2 * npair
    prev = lambda col: pl.BlockSpec((tq, LANES), lambda p, i: (jnp.maximum(i - 1, 0), col + p))
    return pl.pallas_call(
        functools.partial(_sb_prompt_kernel, kcol=kcol, vcol=vcol),
        grid=(npair, s // tq),
        in_specs=[pl.BlockSpec((tq, LANES), lambda p, i: (i, p)),
                  pl.BlockSpec((tq, LANES), lambda p, i: (i, kcol + p)),
                  pl.BlockSpec((tq, LANES), lambda p, i: (i, vcol + p)),
                  prev(kcol), prev(vcol),
                  pl.BlockSpec(memory_space=pl.ANY)],
        out_specs=pl.BlockSpec((tq, LANES), lambda p, i: (i, p)),
        out_shape=jax.ShapeDtypeStruct((s, ATT_W), BF16),
        scratch_shapes=[pltpu.VMEM((tq, LANES), BF16), pltpu.VMEM((tq, LANES), BF16),
                        pltpu.SemaphoreType.DMA((2,))],
        compiler_params=_cparams(("arbitrary", "arbitrary")),
        name="sb_prompt",
    )(qkv, qkv, qkv, qkv, qkv, qkv)


def _sb_sample_kernel(q_ref, kn_ref, vn_ref, kp_ref, vp_ref, ck_hbm, cv_hbm, o_ref, kbuf, vbuf, sem, *, layer):
    batch = pl.program_id(0)
    lq = q_ref.shape[0]
    bk = kp_ref.shape[2]
    q, kn, vn = q_ref[...], kn_ref[...], vn_ref[...]
    causal = lax.broadcasted_iota(I32, (lq, lq), 1) < lax.broadcasted_iota(I32, (lq, lq), 0)
    u_new, u_old = _suffix_ones(lq), _suffix_ones(bk)
    qs = [_head(q, h) for h in range(SB_HEADS)]
    cs, accs = [], []
    for h in range(SB_HEADS):
        pv_n, c = _sb_block(qs[h], _head(kn, h), _head(vn, h), jnp.zeros((lq, 1), F32), u_new, causal=causal)
        pv_p, c = _sb_block(qs[h], kp_ref[h].astype(BF16), vp_ref[h].astype(BF16), c, u_old, seq_minor=True)
        cs.append(c)
        accs.append(pv_n + pv_p)

    def fetch(b):
        cols = pl.ds(pl.multiple_of(b * bk, bk), bk)
        _fetch_blocks([pltpu.make_async_copy(ck_hbm.at[layer, batch, :, :, cols], kbuf, sem.at[0]),
                       pltpu.make_async_copy(cv_hbm.at[layer, batch, :, :, cols], vbuf, sem.at[1])])
        return kbuf, vbuf

    def step(h, kb, vb, c):
        return _sb_block(qs[h], kb[h].astype(BF16), vb[h].astype(BF16), c, u_old, seq_minor=True)

    n_old = ck_hbm.shape[4] // bk
    accs = _sb_older_blocks(n_old - 2, fetch, step, cs, accs)
    o_ref[...] = jnp.concatenate(accs, axis=1).astype(o_ref.dtype)


def _seq_minor(cache):
    return jnp.transpose(cache, (0, 1, 3, 4, 2))


def _sb_sample(qkv, cache_k, cache_v, layer, lq, bk):
    _, nb, nh, hd, past = cache_k.shape
    assert past % bk == 0
    last = past // bk - 1
    newest = pl.BlockSpec((None, None, nh, hd, bk), lambda b: (layer, b, 0, 0, last))
    return pl.pallas_call(
        functools.partial(_sb_sample_kernel, layer=layer),
        grid=(nb,),
        in_specs=[pl.BlockSpec((lq, ATT_W), lambda b: (b, 0)),
                  pl.BlockSpec((lq, ATT_W), lambda b: (b, 1)),
                  pl.BlockSpec((lq, ATT_W), lambda b: (b, 2)),
                  newest, newest,
                  pl.BlockSpec(memory_space=pl.ANY), pl.BlockSpec(memory_space=pl.ANY)],
        out_specs=pl.BlockSpec((lq, ATT_W), lambda b: (b, 0)),
        out_shape=jax.ShapeDtypeStruct((nb * lq, ATT_W), BF16),
        scratch_shapes=[pltpu.VMEM((nh, hd, bk), cache_k.dtype), pltpu.VMEM((nh, hd, bk), cache_v.dtype),
                        pltpu.SemaphoreType.DMA((2,))],
        compiler_params=_cparams(("arbitrary",)),
        name="sb_sample",
    )(qkv, qkv, qkv, cache_k, cache_v, cache_k, cache_v)


def _softmax_pv(s_blocks, v_blocks):
    m = functools.reduce(jnp.maximum, [jnp.max(s, axis=1, keepdims=True) for s in s_blocks])
    p_blocks = [jnp.exp(s - m) for s in s_blocks]
    denom = functools.reduce(jnp.add, [jnp.sum(p, axis=1, keepdims=True) for p in p_blocks])
    pv = functools.reduce(jnp.add, [_dot(p.astype(BF16), v) for p, v in zip(p_blocks, v_blocks)])
    return pv / denom


def _cb_kernel(q_ref, *refs, sizes, lead_blocks):
    nblk = len(sizes)
    k_refs, v_refs = refs[:nblk], refs[nblk:2 * nblk]
    bias_ref, o_ref = refs[2 * nblk], refs[2 * nblk + 1]
    tq = q_ref.shape[0]
    q = q_ref[...]
    first_head = _head_select((tq, LANES))
    ks = [r[...].astype(BF16) for r in k_refs]
    vs = [r[...].astype(BF16) for r in v_refs]
    offs = np.concatenate([[0], np.cumsum(sizes)])
    outs = []
    for head in range(2):
        qa = jnp.where(first_head if head == 0 else ~first_head, q, jnp.zeros_like(q))
        s_blocks = []
        for n in range(nblk):
            b = bias_ref[head, :, int(offs[n]):int(offs[n + 1])]
            s = _dot_nt(qa, ks[n]) + b
            if n < lead_blocks:
                s = jnp.where(pl.program_id(1) >= lead_blocks - n, s, NEG_INF)
            s_blocks.append(s)
        outs.append(_softmax_pv(s_blocks, vs))
    o_ref[...] = jnp.where(first_head, outs[0], outs[1]).astype(o_ref.dtype)


def _rel_bias(rel_table, lq, lk, q_off):
    m = lq + lk
    pos = jnp.arange(m, dtype=I32)
    j_minus_i = jnp.where(pos < lk, pos, pos - m)
    diag = rel_table[:, jnp.clip(q_off - j_minus_i, -REL_CLIP, REL_CLIP) + REL_CLIP].astype(F32)
    h = rel_table.shape[0]
    tiled = jnp.broadcast_to(diag[:, None, :], (h, lq, m)).reshape(h, lq * m)[:, :lq * (m - 1)]
    return tiled.reshape(h, lq, m - 1)[:, :, :lk]


def _cb_prompt(qkv, rel_table, tq):
    s = qkv.shape[0]
    npair = CB_HEADS // 2
    lead = (CB_LEFT_CHUNKS * CHUNK) // tq
    nblk = lead + 1
    q_pos = lead * tq + jnp.arange(tq, dtype=I32)
    k_pos = jnp.arange(nblk * tq, dtype=I32)
    qc, kc = q_pos // CHUNK, k_pos // CHUNK
    band = (kc[None, :] >= qc[:, None] - CB_LEFT_CHUNKS) & (kc[None, :] <= qc[:, None])
    bias = jnp.where(band[None], _rel_bias(rel_table, tq, nblk * tq, lead * tq), NEG_INF)
    base = 3 * npair
    kv_specs = [pl.BlockSpec((tq, LANES), functools.partial(
        lambda p, i, back, col: (jnp.maximum(i - back, 0), col + p), back=lead - n, col=base + off))
        for off in (npair, 2 * npair) for n in range(nblk)]
    return pl.pallas_call(
        functools.partial(_cb_kernel, sizes=(tq,) * nblk, lead_blocks=lead),
        grid=(npair, s // tq),
        in_specs=[pl.BlockSpec((tq, LANES), lambda p, i: (i, base + p))] + kv_specs
        + [pl.BlockSpec((2, tq, nblk * tq), lambda p, i: (p, 0, 0))],
        out_specs=pl.BlockSpec((tq, LANES), lambda p, i: (i, p)),
        out_shape=jax.ShapeDtypeStruct((s, ATT_W), BF16),
        compiler_params=_cparams(("parallel", "arbitrary")),
        name="cb_prompt",
    )(qkv, *([qkv] * (2 * nblk)), bias)


def _cb_sample_kernel(q_ref, kn_ref, vn_ref, ck_ref, cv_ref, bias_ref, o_ref):
    q, kn, vn = q_ref[...], kn_ref[...], vn_ref[...]
    cb_past = ck_ref.shape[2]
    outs = []
    for h in range(CB_HEADS):
        qh = _head(q, h)
        s_old = _dot(qh, ck_ref[h].astype(BF16)) + bias_ref[h, :, :cb_past]
        s_new = _dot_nt(qh, _head(kn, h)) + bias_ref[h, :, cb_past:]
        m = jnp.maximum(jnp.max(s_old, axis=1, keepdims=True), jnp.max(s_new, axis=1, keepdims=True))
        p_old, p_new = jnp.exp(s_old - m), jnp.exp(s_new - m)
        denom = jnp.sum(p_old, axis=1, keepdims=True) + jnp.sum(p_new, axis=1, keepdims=True)
        pv = _dot_nt(p_old.astype(BF16), cv_ref[h].astype(BF16)) + _dot(p_new.astype(BF16), _head(vn, h))
        outs.append(pv / denom)
    o_ref[...] = jnp.concatenate(outs, axis=1).astype(o_ref.dtype)


def _cb_sample(qkv, cache_k, cache_v, rel_table, layer, lq):
    _, nb, nh, hd, cb_past = cache_k.shape
    bias = _rel_bias(rel_table, lq, cb_past + lq, cb_past)
    cache_spec = pl.BlockSpec((None, None, nh, hd, cb_past), lambda b: (layer, b, 0, 0, 0))
    qcol = 3
    return pl.pallas_call(
        _cb_sample_kernel,
        grid=(nb,),
        in_specs=[pl.BlockSpec((lq, ATT_W), lambda b: (b, qcol)),
                  pl.BlockSpec((lq, ATT_W), lambda b: (b, qcol + 1)),
                  pl.BlockSpec((lq, ATT_W), lambda b: (b, qcol + 2)),
                  cache_spec, cache_spec, _resident(bias)],
        out_specs=pl.BlockSpec((lq, ATT_W), lambda b: (b, 0)),
        out_shape=jax.ShapeDtypeStruct((nb * lq, ATT_W), BF16),
        compiler_params=_cparams(("parallel",)),
        name="cb_sample",
    )(qkv, qkv, qkv, cache_k, cache_v, bias)


def _memkv_kernel(mem_ref, g_ref, w_ref, k_ref, v_ref):
    h = _rms(mem_ref[...], g_ref[...]).astype(BF16)
    kv = _dot(h, w_ref[...])
    half = k_ref.shape[1]
    k_ref[...] = kv[:, :half]
    v_ref[...] = kv[:, half:]


def _memkv(mem, g, w_bf16):
    n = mem.shape[0]
    half = w_bf16.shape[1] // 2
    return pl.pallas_call(
        _memkv_kernel,
        out_shape=[jax.ShapeDtypeStruct((n, half), F32)] * 2,
        compiler_params=pltpu.CompilerParams(vmem_limit_bytes=VMEM_LIMIT),
        name="memkv",
    )(mem, g, w_bf16)


def _xattn_kernel(q_ref, k_ref, v_ref, o_ref):
    q = q_ref[...]
    k = k_ref[...].astype(BF16)
    v = v_ref[...].astype(BF16)
    for h in range(MEM_HEADS):
        cols = slice(h * MEM_HEAD_DIM, (h + 1) * MEM_HEAD_DIM)
        s = _dot_nt(q[:, cols], k[:, cols]) * MEM_SCALE
        o_ref[:, cols] = _softmax_pv([s], [v[:, cols]]).astype(o_ref.dtype)


def _xattn(qkv, mem_k, mem_v, tq, per_tile_kv):
    t = qkv.shape[0]
    qcol = (N_QKV_SEG - 1)
    if per_tile_kv:
        kv_spec = pl.BlockSpec((None,) + mem_k.shape[1:], lambda i: (i, 0, 0))
    else:
        kv_spec = pl.BlockSpec(mem_k.shape, lambda i: (0, 0))
    return pl.pallas_call(
        _xattn_kernel,
        grid=(t // tq,),
        in_specs=[pl.BlockSpec((tq, ATT_W), lambda i: (i, qcol)), kv_spec, kv_spec],
        out_specs=pl.BlockSpec((tq, ATT_W), lambda i: (i, 0)),
        out_shape=jax.ShapeDtypeStruct((t, ATT_W), BF16),
        compiler_params=_cparams(("parallel",)),
        name="xattn",
    )(qkv, mem_k, mem_v)


def _merge_kernel(x_ref, osb_ref, ocb_ref, om_ref, g_ref, wsb_ref, wcb_ref, wm_ref, wo_ref,
                  gf_ref, wr_ref, br_ref, cnt0_ref,
                  y_ref, hf_ref, e_ref, gate_ref, rank_ref, cnt_ref, cnt_scr):
    i = pl.program_id(0)
    tm, d = x_ref.shape

    @pl.when(i == 0)
    def _():
        cnt_scr[...] = cnt0_ref[...].astype(F32)

    m = None
    for n, (o_ref, w_ref) in enumerate(((osb_ref, wsb_ref), (ocb_ref, wcb_ref), (om_ref, wm_ref))):
        term = g_ref[:, n * d:(n + 1) * d].astype(F32) * _dot(o_ref[...], w_ref[...])
        m = term if m is None else m + term
    y = x_ref[...] + _dot(m.astype(BF16), wo_ref[...])
    y_ref[...] = y
    hf = _rms(y, gf_ref[...])
    hf_ref[...] = hf
    logits = _dot(hf.astype(BF16), wr_ref[...]) + br_ref[...]

    ne = logits.shape[1]
    lane = lax.broadcasted_iota(I32, (tm, ne), 1)
    work = logits
    vals, sels, idxs = [], [], []
    for _ in range(TOP_K):
        mx = jnp.max(work, axis=1, keepdims=True)
        idx = jnp.min(jnp.where(work == mx, lane, ne), axis=1, keepdims=True)
        sel = lane == idx
        vals.append(mx)
        idxs.append(idx)
        sels.append(sel)
        work = jnp.where(sel, -jnp.inf, work)
    ex = [jnp.exp(v - vals[0]) for v in vals]
    denom = functools.reduce(jnp.add, ex)

    onehot = functools.reduce(jnp.add, [s.astype(F32) for s in sels])
    before = (lax.broadcasted_iota(I32, (tm, tm), 1) < lax.broadcasted_iota(I32, (tm, tm), 0)).astype(BF16)
    prior = _dot(before, onehot.astype(BF16)) + cnt_scr[...]
    for k in range(TOP_K):
        e_ref[:, k:k + 1] = idxs[k]
        gate_ref[:, k:k + 1] = ex[k] / denom
        rank_ref[:, k:k + 1] = jnp.sum(jnp.where(sels[k], prior, 0.0), axis=1, keepdims=True).astype(I32)
    cnt_scr[...] += jnp.sum(onehot, axis=0, keepdims=True)
    cnt_ref[...] = cnt_scr[...].astype(I32)


def _merge(x, o_sb, o_cb, o_m, gates, w_sb, w_cb, w_m, w_out, g_ffn, w_router, b_router, cnt0, tm):
    t, d = x.shape
    ne = w_router.shape[1]
    row = lambda i: (i, 0)
    return pl.pallas_call(
        _merge_kernel,
        grid=(t // tm,),
        in_specs=[pl.BlockSpec((tm, d), row)] + [pl.BlockSpec((tm, ATT_W), row)] * 3
        + [pl.BlockSpec((tm, gates.shape[1]), row)]
        + [_resident(a) for a in (w_sb, w_cb, w_m, w_out, g_ffn, w_router, b_router, cnt0)],
        out_specs=[pl.BlockSpec((tm, d), row), pl.BlockSpec((tm, d), row)]
        + [pl.BlockSpec((tm, TOP_K), row)] * 3 + [pl.BlockSpec((1, ne), lambda i: (0, 0))],
        out_shape=[jax.ShapeDtypeStruct((t, d), F32), jax.ShapeDtypeStruct((t, d), F32),
                   jax.ShapeDtypeStruct((t, TOP_K), I32), jax.ShapeDtypeStruct((t, TOP_K), F32),
                   jax.ShapeDtypeStruct((t, TOP_K), I32), jax.ShapeDtypeStruct((1, ne), I32)],
        scratch_shapes=[pltpu.VMEM((1, ne), F32)],
        compiler_params=_cparams(("arbitrary",)),
        name="merge_router",
    )(x, o_sb, o_cb, o_m, gates, w_sb, w_cb, w_m, w_out, g_ffn, w_router, b_router, cnt0)


def _row_copies(make_copy, n_rows):
    def start(r, carry):
        for k in range(TOP_K):
            make_copy(r, k).start(priority=k % 2)
        return carry

    def wait(r, carry):
        for k in range(TOP_K):
            make_copy(r, k).wait()
        return carry

    lax.fori_loop(0, n_rows, start, 0, unroll=ROW_UNROLL)
    lax.fori_loop(0, n_rows, wait, 0, unroll=ROW_UNROLL)


def _n_chunks(count):
    return lax.shift_right_logical(count + (EXPERT_ROWS - 1), EXPERT_ROWS.bit_length() - 1)


def _used_rows(start_ref, cnt_ref):
    last = cnt_ref.shape[0] - 1
    return start_ref[last] + _n_chunks(cnt_ref[last]) * EXPERT_ROWS


def _zero_chunks(zeros_ref, dst_hbm, sem, n, chunk_row):
    def copy(c):
        rows = pl.ds(pl.multiple_of(chunk_row(c), EXPERT_ROWS), EXPERT_ROWS)
        return pltpu.make_async_copy(zeros_ref, dst_hbm.at[rows], sem)

    def start(c, carry):
        copy(c).start()
        return carry

    def wait(c, carry):
        copy(c).wait()
        return carry

    lax.fori_loop(0, n, start, 0)
    lax.fori_loop(0, n, wait, 0)


def _zero_tail(zeros_ref, dst_hbm, sem, start_ref, cnt_ref):
    used = _used_rows(start_ref, cnt_ref)
    n_tail = lax.shift_right_logical(dst_hbm.shape[0] - used, EXPERT_ROWS.bit_length() - 1)
    _zero_chunks(zeros_ref, dst_hbm, sem, n_tail, lambda c: used + c * EXPERT_ROWS)


def _dispatch_kernel(dest_ref, start_ref, cnt_ref, *refs, tiles):
    hf_refs, (xs_ref, sem, zbuf, zsem) = refs[:len(tiles)], refs[len(tiles):]
    i = pl.program_id(0)
    tm = hf_refs[0].shape[0]

    @pl.when(i == 0)
    def _():
        zbuf[...] = jnp.zeros_like(zbuf)
        last_row = xs_ref.shape[0] - EXPERT_ROWS

        def last_chunk(e):
            return jnp.minimum(start_ref[e] + jnp.maximum(_n_chunks(cnt_ref[e]) - 1, 0) * EXPERT_ROWS, last_row)

        _zero_chunks(zbuf, xs_ref, zsem, cnt_ref.shape[0], last_chunk)
        _zero_tail(zbuf, xs_ref, zsem, start_ref, cnt_ref)

    first_tile = 0
    for hf_ref, n_tiles in zip(hf_refs, tiles):
        @pl.when(jnp.logical_and(i >= first_tile, i < first_tile + n_tiles))
        def _(hf_ref=hf_ref):
            def row_copy(r, k):
                dst = dest_ref[(i * tm + r) * TOP_K + k]
                return pltpu.make_async_copy(hf_ref.at[pl.ds(r, 1)], xs_ref.at[pl.ds(dst, 1)], sem)

            _row_copies(row_copy, tm)
        first_tile += n_tiles


def _dispatch(dest_flat, row_start, counts, hfs, rows, tm):
    d = hfs[0].shape[1]
    tiles = tuple(hf.shape[0] // tm for hf in hfs)
    firsts = np.concatenate([[0], np.cumsum(tiles)[:-1]])
    specs = [pl.BlockSpec((tm, d), functools.partial(
        lambda i, *_, first, n: (jnp.clip(i - first, 0, n - 1), 0), first=int(f), n=n))
        for f, n in zip(firsts, tiles)]
    return pl.pallas_call(
        functools.partial(_dispatch_kernel, tiles=tiles),
        grid_spec=pltpu.PrefetchScalarGridSpec(
            num_scalar_prefetch=3,
            grid=(sum(tiles),),
            in_specs=specs,
            out_specs=pl.BlockSpec(memory_space=pl.ANY),
            scratch_shapes=[pltpu.SemaphoreType.DMA, pltpu.VMEM((EXPERT_ROWS, d), hfs[0].dtype),
                            pltpu.SemaphoreType.DMA]),
        out_shape=jax.ShapeDtypeStruct((rows, d), hfs[0].dtype),
        compiler_params=_cparams(("arbitrary",)),
        name="moe_dispatch",
    )(dest_flat, row_start, counts, *hfs)


def _expert_kernel(start_ref, cnt_ref, xs_hbm, wup_ref, wd_ref, bg_ref, bu_ref, bd_ref, ys_hbm,
                   wg_scr, wu_scr, wd_scr, xbuf, ybuf, sem_in, sem_out):
    e = pl.program_id(0)
    row0 = start_ref[e]
    n = _n_chunks(cnt_ref[e])

    def rows(j):
        return pl.ds(pl.multiple_of(row0 + j * EXPERT_ROWS, EXPERT_ROWS), EXPERT_ROWS)

    def in_copy(j, slot):
        return pltpu.make_async_copy(xs_hbm.at[rows(j)], xbuf.at[slot], sem_in.at[slot])

    def out_copy(j, slot):
        return pltpu.make_async_copy(ybuf.at[slot], ys_hbm.at[rows(j)], sem_out.at[slot])

    @pl.when(n > 0)
    def _():
        in_copy(0, 0).start()
        grp = 2 * LANES
        src = lax.broadcasted_iota(I32, (grp, grp), 0)
        dst = lax.broadcasted_iota(I32, (grp, grp), 1)
        perm = (src == jnp.where(dst < LANES, 2 * dst, 2 * (dst - LANES) + 1)).astype(BF16)
        for c in range(wup_ref.shape[1] // grp):
            w = _dot(wup_ref[:, c * grp:(c + 1) * grp].astype(BF16), perm).astype(BF16)
            wg_scr[:, c * LANES:(c + 1) * LANES] = w[:, :LANES]
            wu_scr[:, c * LANES:(c + 1) * LANES] = w[:, LANES:]
        wd_scr[...] = wd_ref[...].astype(BF16)

        def chunk(j, carry):
            slot = lax.rem(j, 2)
            in_copy(j, slot).wait()

            @pl.when(j + 1 < n)
            def _():
                in_copy(j + 1, 1 - slot).start()

            @pl.when(j >= 2)
            def _():
                out_copy(j - 2, slot).wait()

            x = xbuf[slot].astype(BF16)
            g = jnp.minimum(_dot(x, wg_scr[...]) + bg_ref[...], SWIGLU_LIMIT)
            u = jnp.clip(_dot(x, wu_scr[...]) + bu_ref[...], -SWIGLU_LIMIT, SWIGLU_LIMIT)
            act = (u + 1.0) * (g * jax.nn.sigmoid(SWIGLU_ALPHA * g))
            ybuf[slot] = _dot(act.astype(BF16), wd_scr[...]) + bd_ref[...]
            out_copy(j, slot).start()
            return carry

        lax.fori_loop(0, n, chunk, 0)

        @pl.when(n >= 2)
        def _():
            out_copy(n - 2, lax.rem(n, 2)).wait()

        out_copy(n - 1, lax.rem(n - 1, 2)).wait()

    @pl.when(e == pl.num_programs(0) - 1)
    def _():
        ybuf[0] = jnp.zeros(ybuf.shape[1:], ybuf.dtype)
        _zero_tail(ybuf.at[0], ys_hbm, sem_out.at[0], start_ref, cnt_ref)


def _experts(row_start, counts, xs, w_up, w_d, b_g, b_u, b_d):
    rows, d = xs.shape
    ne, f = w_d.shape[0], w_d.shape[1]
    wspec = lambda a: pl.BlockSpec((None,) + a.shape[1:], lambda e, *_: (e, 0, 0))
    hbm = pl.BlockSpec(memory_space=pl.ANY)
    return pl.pallas_call(
        _expert_kernel,
        grid_spec=pltpu.PrefetchScalarGridSpec(
            num_scalar_prefetch=2,
            grid=(ne,),
            in_specs=[hbm] + [wspec(a) for a in (w_up, w_d, b_g, b_u, b_d)],
            out_specs=hbm,
            scratch_shapes=[pltpu.VMEM((d, f), BF16), pltpu.VMEM((d, f), BF16), pltpu.VMEM((f, d), BF16),
                            pltpu.VMEM((2, EXPERT_ROWS, d), xs.dtype), pltpu.VMEM((2, EXPERT_ROWS, d), F32),
                            pltpu.SemaphoreType.DMA((2,)), pltpu.SemaphoreType.DMA((2,))]),
        out_shape=jax.ShapeDtypeStruct((rows, d), F32),
        compiler_params=_cparams(("arbitrary",)),
        name="moe_experts",
    )(row_start, counts, xs, w_up, w_d, b_g, b_u, b_d)


def _combine_kernel(dest_ref, y_ref, gate_ref, gfin_ref, ys_ref, o_ref, buf, sem):
    i = pl.program_id(0)
    tm = y_ref.shape[0]

    def row_copy(r, k):
        src = dest_ref[(i * tm + r) * TOP_K + k]
        return pltpu.make_async_copy(ys_ref.at[pl.ds(src, 1)], buf.at[k, pl.ds(r, 1)], sem)

    _row_copies(row_copy, tm)
    moe =functools.reduce(jnp.add, [buf[k] * gate_ref[:, k:k + 1] for k in range(TOP_K)])
    o_ref[...] = _rms(y_ref[...] + moe, gfin_ref[...])


def _combine(dest_flat, y, gate, g_final, ys, tm):
    t, d = y.shape
    return pl.pallas_call(
        _combine_kernel,
        grid_spec=pltpu.PrefetchScalarGridSpec(
            num_scalar_prefetch=1,
            grid=(t // tm,),
            in_specs=[pl.BlockSpec((tm, d), lambda i, dest: (i, 0)),
                      pl.BlockSpec((tm, TOP_K), lambda i, dest: (i, 0)),
                      pl.BlockSpec((1, d), lambda i, dest: (0, 0)),
                      pl.BlockSpec(memory_space=pl.ANY)],
            out_specs=pl.BlockSpec((tm, d), lambda i, dest: (i, 0)),
            scratch_shapes=[pltpu.VMEM((TOP_K, tm, d), F32), pltpu.SemaphoreType.DMA]),
        out_shape=jax.ShapeDtypeStruct((t, d), F32),
        compiler_params=_cparams(("arbitrary",)),
        name="moe_combine",
    )(dest_flat, y, gate, g_final, ys)


def _common_tile(sizes, pref):
    t = pref
    while any(s % t for s in sizes):
        t //= 2
    return t


def _moe(groups, counts, w_up, w_d, b_g, b_u, b_d, g_final):
    ne = counts.shape[1]
    n_pairs = sum(g[0].shape[0] for g in groups) * TOP_K
    rows = (n_pairs + ne * (EXPERT_ROWS - 1) + EXPERT_ROWS - 1) // EXPERT_ROWS * EXPERT_ROWS
    counts = counts[0]
    chunks = (counts + EXPERT_ROWS - 1) // EXPERT_ROWS
    eid = jnp.arange(ne, dtype=I32)
    row_start = jnp.sum(jnp.where(eid[None, :] < eid[:, None], chunks[None, :], 0), axis=1) * EXPERT_ROWS
    dests = [(jnp.sum(jnp.where(top_e[..., None] == eid, row_start, 0), axis=-1) + rank).reshape(-1)
             for _, _, top_e, _, rank in groups]
    sizes = [g[0].shape[0] for g in groups]
    xs = _dispatch(jnp.concatenate(dests), row_start, counts, [g[1] for g in groups], rows,
                   _common_tile(sizes, DISPATCH_ROWS))
    ys = _experts(row_start, counts, xs, w_up, w_d, b_g, b_u, b_d)
    return [_combine(dest, y, gate, g_final, ys, _common_tile([y.shape[0]], COMBINE_ROWS))
            for (y, _, _, gate, _), dest in zip(groups, dests)]


def _tile(n, pref):
    return pref if n % pref == 0 else n


def _attn_block(x, attend, weights, cnt0, tm):
    g_attn, w_in, b_gate, w_sb, w_cb, w_m, w_out, g_ffn, w_router, b_router = weights
    qkv, ksb, vsb, kcb, vcb, gates = _inproj(x, g_attn, w_in, b_gate, tm)
    o_sb, o_cb, o_m = attend(qkv)
    routed = _merge(x, o_sb, o_cb, o_m, gates, w_sb, w_cb, w_m, w_out, g_ffn, w_router, b_router, cnt0, tm)
    return routed, (ksb, vsb, kcb, vcb)


def kernel(x_prompt, x_sample, mem_prompt, cache_sb_k, cache_sb_v, cache_cb_k, cache_cb_v, cache_mem_k, cache_mem_v, g_attn, w_in, b_gate, rel_table, g_mem, w_mem_kv, w_br_sb, w_br_cb, w_br_mem, w_out, g_ffn, w_router, b_router, w_up, b_up, w_down, b_down, g_final):
    depth = w_in.shape[0]
    assert depth == 1, "one layer: the per-group outputs below are the final residual streams"
    bp, seq, d = x_prompt.shape
    nb, lq, _ = x_sample.shape
    assert bp == 1
    past = cache_sb_k.shape[2]
    band_keep = min(CB_LEFT_CHUNKS * CHUNK, seq)
    l = 0
    row2 = lambda a: a.reshape(1, -1)
    weights = (row2(g_attn[l]), w_in[l].astype(BF16), row2(b_gate[l]),
               w_br_sb[l].astype(BF16), w_br_cb[l].astype(BF16), w_br_mem[l].astype(BF16),
               w_out[l].astype(BF16), row2(g_ffn[l]), w_router[l].astype(BF16), row2(b_router[l]))

    mk, mv = _memkv(mem_prompt[0], row2(g_mem[l]), w_mem_kv[l].astype(BF16))

    def attend_prompt(qkv):
        return (_sb_prompt(qkv, _tile(seq, 256)), _cb_prompt(qkv, rel_table[l], _tile(seq, 256)),
                _xattn(qkv, mk, mv, _tile(seq, 512), per_tile_kv=False))

    def attend_sample(qkv):
        flat = lambda c: c.reshape(c.shape[0], c.shape[1], -1)
        return (_sb_sample(qkv, _seq_minor(cache_sb_k), _seq_minor(cache_sb_v), l, lq, _tile(past, 256)),
                _cb_sample(qkv, _seq_minor(cache_cb_k), _seq_minor(cache_cb_v), rel_table[l], l, lq),
                _xattn(qkv, flat(cache_mem_k[l]), flat(cache_mem_v[l]), lq, per_tile_kv=True))

    no_tokens = jnp.zeros((1, w_router.shape[2]), I32)
    routed_p, (ksb_p, vsb_p, kcb_p, vcb_p) = _attn_block(
        x_prompt.reshape(seq, d), attend_prompt, weights, no_tokens, _tile(seq, 512))
    routed_s, (ksb_s, vsb_s, kcb_s, vcb_s) = _attn_block(
        x_sample.reshape(nb * lq, d), attend_sample, weights, routed_p[5], _tile(nb * lq, 512))
    yp, ys = _moe([routed_p[:5], routed_s[:5]], routed_s[5], w_up[l], w_down[l],
                  b_up[l][:, None, 0::2], b_up[l][:, None, 1::2], b_down[l][:, None, :], row2(g_final))

    heads = lambda a, b, n, h, hd: a.reshape(1, b, n, h, hd)
    return (yp.reshape(1, seq, d), ys.reshape(nb, lq, d),
            heads(ksb_p, 1, seq, SB_HEADS, HEAD_DIM), heads(vsb_p, 1, seq, SB_HEADS, HEAD_DIM),
            heads(kcb_p[seq - band_keep:], 1, band_keep, CB_HEADS, HEAD_DIM),
            heads(vcb_p[seq - band_keep:], 1, band_keep, CB_HEADS, HEAD_DIM),
            heads(mk, 1, N_MEM, MEM_HEADS, MEM_HEAD_DIM), heads(mv, 1, N_MEM, MEM_HEADS, MEM_HEAD_DIM),
            heads(ksb_s, nb, lq, SB_HEADS, HEAD_DIM), heads(vsb_s, nb, lq, SB_HEADS, HEAD_DIM),
            heads(kcb_s, nb, lq, CB_HEADS, HEAD_DIM), heads(vcb_s, nb, lq, CB_HEADS, HEAD_DIM))
```

```python
import functools

import jax
import jax.numpy as jnp
import numpy as np
from jax import lax
from jax.experimental import pallas as pl
from jax.experimental.pallas import tpu as pltpu

F32 = jnp.float32
BF16 = jnp.bfloat16
I32 = jnp.int32

CHUNK = 64
N_MEM = 256
SB_HEADS = 8
CB_HEADS = 8
HEAD_DIM = 64
MEM_HEADS = 4
MEM_HEAD_DIM = 128
CB_LEFT_CHUNKS = 8
REL_CLIP = 128
ATT_W = 512
N_QKV_SEG = 7
N_EXPERTS = 32
TOP_K = 4
SWIGLU_LIMIT = 7.0
SWIGLU_ALPHA = 1.702
RMS_EPS = 1e-6
NEG_INF = -1e30

LANES = 128
VMEM_LIMIT = 56 * 1024 * 1024
SB_UNDERFLOW = 104.0
LOG2_E = 1.4426950408889634

EXPERT_ROWS = 256
CHUNK_DMA_PRIORITY = 1
DISPATCH_ROWS = 256
COMBINE_ROWS = 512
ROW_UNROLL = 8

QK_SCALE = HEAD_DIM ** -0.5
MEM_SCALE = MEM_HEAD_DIM ** -0.5


def _cparams(sem):
    return pltpu.CompilerParams(dimension_semantics=sem, vmem_limit_bytes=VMEM_LIMIT)


def _rms(x, g):
    return x * lax.rsqrt(jnp.mean(x * x, axis=-1, keepdims=True) + RMS_EPS) * g


def _dot(a, b):
    return jnp.dot(a, b, preferred_element_type=F32)


def _dot_nt(a, b):
    return lax.dot_general(a, b, (((1,), (1,)), ((), ())), preferred_element_type=F32)


def _inproj_kernel(x_ref, g_ref, w_ref, bg_ref, qkv_ref, ksb_ref, vsb_ref, kcb_ref, vcb_ref, gate_ref, h_scr):
    h_scr[...] = _rms(x_ref[...], g_ref[...]).astype(BF16)
    f32_out = {1: ksb_ref, 2: vsb_ref, 4: kcb_ref, 5: vcb_ref}
    for seg in range(w_ref.shape[1] // ATT_W):
        cols = slice(seg * ATT_W, (seg + 1) * ATT_W)
        acc = _dot(h_scr[...], w_ref[:, cols])
        if seg < N_QKV_SEG:
            qkv_ref[:, cols] = (acc * (QK_SCALE if seg in (0, 3) else 1.0)).astype(BF16)
            if seg in f32_out:
                f32_out[seg][...] = acc
        else:
            gcols = slice((seg - N_QKV_SEG) * ATT_W, (seg - N_QKV_SEG + 1) * ATT_W)
            gate_ref[:, gcols] = jax.nn.sigmoid(acc + bg_ref[:, gcols]).astype(BF16)


def _resident(a):
    return pl.BlockSpec(a.shape, lambda *_: (0,) * a.ndim, pipeline_mode=pl.Buffered(1))


def _inproj(x, g, w_bf16, b_gate, tm):
    t, d = x.shape
    n_gate = w_bf16.shape[1] - N_QKV_SEG * ATT_W
    row = lambda i: (i, 0)
    f32_spec = pl.BlockSpec((tm, ATT_W), row)
    return pl.pallas_call(
        _inproj_kernel,
        grid=(t // tm,),
        in_specs=[pl.BlockSpec((tm, d), row), _resident(g), _resident(w_bf16), _resident(b_gate)],
        out_specs=[pl.BlockSpec((tm, N_QKV_SEG * ATT_W), row), f32_spec, f32_spec, f32_spec, f32_spec,
                   pl.BlockSpec((tm, n_gate), row)],
        out_shape=[jax.ShapeDtypeStruct((t, N_QKV_SEG * ATT_W), BF16)]
        + [jax.ShapeDtypeStruct((t, ATT_W), F32)] * 4
        + [jax.ShapeDtypeStruct((t, n_gate), BF16)],
        scratch_shapes=[pltpu.VMEM((tm, d), BF16)],
        compiler_params=_cparams(("parallel",)),
        name="inproj",
    )(x, g, w_bf16, b_gate)


def _head(a, h):
    return a[:, h * HEAD_DIM:(h + 1) * HEAD_DIM]


def _head_select(shape):
    return lax.broadcasted_iota(I32, shape, 1) < HEAD_DIM


def _suffix_ones(n):
    return (lax.broadcasted_iota(I32, (n, n), 0) > lax.broadcasted_iota(I32, (n, n), 1)).astype(BF16)


def _sb_block(q, kb, vb, c, u, causal=None, valid=None, seq_minor=False):
    z2 = (_dot(q, kb) if seq_minor else _dot_nt(q, kb)) * LOG2_E
    sp = jnp.maximum(z2, 0.0) + jnp.log2(1.0 + jnp.exp2(-jnp.abs(z2)))
    keep = causal if valid is None else (valid if causal is None else jnp.logical_and(causal, valid))
    l1m = -sp if keep is None else jnp.where(keep, -sp, 0.0)
    hi = l1m.astype(BF16)
    lo = (l1m - hi.astype(F32)).astype(BF16)
    tail = _dot(hi, u) + _dot(lo, u)
    a = jnp.exp2((z2 - sp) + tail + c)
    if keep is not None:
        a = jnp.where(keep, a, 0.0)
    pv = _dot_nt(a.astype(BF16), vb) if seq_minor else _dot(a.astype(BF16), vb)
    return pv, c + jnp.sum(l1m, axis=1, keepdims=True)


def _any_live(cs):
    return (functools.reduce(jnp.maximum, [jnp.max(c) for c in cs]) > -SB_UNDERFLOW * LOG2_E).astype(I32)


def _sb_older_blocks(first_block, fetch, step, cs, accs):
    n = len(cs)

    def cond(carry):
        return jnp.logical_and(carry[0] >= 0, carry[1] > 0)

    def body(carry):
        b, cs, accs = carry[0], carry[2:2 + n], carry[2 + n:]
        kb, vb = fetch(b)
        res = [step(s, kb, vb, cs[s]) for s in range(n)]
        cs = [r[1] for r in res]
        return (b - 1, _any_live(cs), *cs, *[acc + r[0] for acc, r in zip(accs, res)])

    out = lax.while_loop(cond, body, (first_block, _any_live(cs), *cs, *accs))
    return out[2 + n:]


def _fetch_blocks(copies):
    for c in copies:
        c.start()
    for c in copies:
        c.wait()


def _sb_prompt_kernel(q_ref, kd_ref, vd_ref, kp_ref, vp_ref, qkv_hbm, o_ref, kbuf, vbuf, sem, *, kcol, vcol):
    p, i = pl.program_id(0), pl.program_id(1)
    tq = q_ref.shape[0]
    q = q_ref[...]
    first_head = _head_select((tq, LANES))
    qs = [jnp.where(first_head, q, jnp.zeros_like(q)), jnp.where(first_head, jnp.zeros_like(q), q)]
    causal = lax.broadcasted_iota(I32, (tq, tq), 1) < lax.broadcasted_iota(I32, (tq, tq), 0)
    u = _suffix_ones(tq)
    kd, vd, kp, vp = kd_ref[...], vd_ref[...], kp_ref[...], vp_ref[...]
    cs, accs = [], []
    for qa in qs:
        pv_d, c = _sb_block(qa, kd, vd, jnp.zeros((tq, 1), F32), u, causal=causal)
        pv_p, c = _sb_block(qa, kp, vp, c, u, valid=i > 0)
        cs.append(c)
        accs.append(pv_d + pv_p)

    def fetch(b):
        rows = pl.ds(pl.multiple_of(b * tq, tq), tq)
        _fetch_blocks([
            pltpu.make_async_copy(qkv_hbm.at[rows, pl.ds(pl.multiple_of((kcol + p) * LANES, LANES), LANES)],
                                  kbuf, sem.at[0]),
            pltpu.make_async_copy(qkv_hbm.at[rows, pl.ds(pl.multiple_of((vcol + p) * LANES, LANES), LANES)],
                                  vbuf, sem.at[1])])
        return kbuf[...], vbuf[...]

    accs = _sb_older_blocks(i - 2, fetch, lambda s, kb, vb, c: _sb_block(qs[s], kb, vb, c, u), cs, accs)
    o_ref[...] = jnp.where(first_head, accs[0], accs[1]).astype(o_ref.dtype)


def _sb_prompt(qkv, tq):
    s = qkv.shape[0]
    npair = SB_HEADS // 2
    kcol, vcol = npair, 2 * npair
    prev = lambda col: pl.BlockSpec((tq, LANES), lambda p, i: (jnp.maximum(i - 1, 0), col + p))
    return pl.pallas_call(
        functools.partial(_sb_prompt_kernel, kcol=kcol, vcol=vcol),
        grid=(npair, s // tq),
        in_specs=[pl.BlockSpec((tq, LANES), lambda p, i: (i, p)),
                  pl.BlockSpec((tq, LANES), lambda p, i: (i, kcol + p)),
                  pl.BlockSpec((tq, LANES), lambda p, i: (i, vcol + p)),
                  prev(kcol), prev(vcol),
                  pl.BlockSpec(memory_space=pl.ANY)],
        out_specs=pl.BlockSpec((tq, LANES), lambda p, i: (i, p)),
        out_shape=jax.ShapeDtypeStruct((s, ATT_W), BF16),
        scratch_shapes=[pltpu.VMEM((tq, LANES), BF16), pltpu.VMEM((tq, LANES), BF16),
                        pltpu.SemaphoreType.DMA((2,))],
        compiler_params=_cparams(("arbitrary", "arbitrary")),
        name="sb_prompt",
    )(qkv, qkv, qkv, qkv, qkv, qkv)


def _sb_sample_kernel(q_ref, kn_ref, vn_ref, kp_ref, vp_ref, ck_hbm, cv_hbm, o_ref, kbuf, vbuf, sem, *, layer):
    batch = pl.program_id(0)
    lq = q_ref.shape[0]
    bk = kp_ref.shape[2]
    q, kn, vn = q_ref[...], kn_ref[...], vn_ref[...]
    causal = lax.broadcasted_iota(I32, (lq, lq), 1) < lax.broadcasted_iota(I32, (lq, lq), 0)
    u_new, u_old = _suffix_ones(lq), _suffix_ones(bk)
    qs = [_head(q, h) for h in range(SB_HEADS)]
    cs, accs = [], []
    for h in range(SB_HEADS):
        pv_n, c = _sb_block(qs[h], _head(kn, h), _head(vn, h), jnp.zeros((lq, 1), F32), u_new, causal=causal)
        pv_p, c = _sb_block(qs[h], kp_ref[h].astype(BF16), vp_ref[h].astype(BF16), c, u_old, seq_minor=True)
        cs.append(c)
        accs.append(pv_n + pv_p)

    def fetch(b):
        cols = pl.ds(pl.multiple_of(b * bk, bk), bk)
        _fetch_blocks([pltpu.make_async_copy(ck_hbm.at[layer, batch, :, :, cols], kbuf, sem.at[0]),
                       pltpu.make_async_copy(cv_hbm.at[layer, batch, :, :, cols], vbuf, sem.at[1])])
        return kbuf, vbuf

    def step(h, kb, vb, c):
        return _sb_block(qs[h], kb[h].astype(BF16), vb[h].astype(BF16), c, u_old, seq_minor=True)

    n_old = ck_hbm.shape[4] // bk
    accs = _sb_older_blocks(n_old - 2, fetch, step, cs, accs)
    o_ref[...] = jnp.concatenate(accs, axis=1).astype(o_ref.dtype)


def _seq_minor(cache):
    return jnp.transpose(cache, (0, 1, 3, 4, 2))


def _sb_sample(qkv, cache_k, cache_v, layer, lq, bk):
    _, nb, nh, hd, past = cache_k.shape
    assert past % bk == 0
    last = past // bk - 1
    newest = pl.BlockSpec((None, None, nh, hd, bk), lambda b: (layer, b, 0, 0, last))
    return pl.pallas_call(
        functools.partial(_sb_sample_kernel, layer=layer),
        grid=(nb,),
        in_specs=[pl.BlockSpec((lq, ATT_W), lambda b: (b, 0)),
                  pl.BlockSpec((lq, ATT_W), lambda b: (b, 1)),
                  pl.BlockSpec((lq, ATT_W), lambda b: (b, 2)),
                  newest, newest,
                  pl.BlockSpec(memory_space=pl.ANY), pl.BlockSpec(memory_space=pl.ANY)],
        out_specs=pl.BlockSpec((lq, ATT_W), lambda b: (b, 0)),
        out_shape=jax.ShapeDtypeStruct((nb * lq, ATT_W), BF16),
        scratch_shapes=[pltpu.VMEM((nh, hd, bk), cache_k.dtype), pltpu.VMEM((nh, hd, bk), cache_v.dtype),
                        pltpu.SemaphoreType.DMA((2,))],
        compiler_params=_cparams(("arbitrary",)),
        name="sb_sample",
    )(qkv, qkv, qkv, cache_k, cache_v, cache_k, cache_v)


def _softmax_pv(s_blocks, v_blocks):
    m = jnp.max(functools.reduce(jnp.maximum, s_blocks), axis=1, keepdims=True)
    p_blocks = [jnp.exp(s - m) for s in s_blocks]
    denom = jnp.sum(functools.reduce(jnp.add, p_blocks), axis=1, keepdims=True)
    pv = functools.reduce(jnp.add, [_dot(p.astype(BF16), v) for p, v in zip(p_blocks, v_blocks)])
    return pv / denom


def _cb_kernel(q_ref, *refs, sizes, lead_blocks):
    nblk = len(sizes)
    k_refs, v_refs = refs[:nblk], refs[nblk:2 * nblk]
    bias_ref, o_ref = refs[2 * nblk], refs[2 * nblk + 1]
    tq = q_ref.shape[0]
    q = q_ref[...]
    first_head = _head_select((tq, LANES))
    ks = [r[...].astype(BF16) for r in k_refs]
    vs = [r[...].astype(BF16) for r in v_refs]
    offs = np.concatenate([[0], np.cumsum(sizes)])
    outs = []
    for head in range(2):
        qa = jnp.where(first_head if head == 0 else ~first_head, q, jnp.zeros_like(q))
        s_blocks = []
        for n in range(nblk):
            b = bias_ref[head, :, int(offs[n]):int(offs[n + 1])]
            s = _dot_nt(qa, ks[n]) + b
            if n < lead_blocks:
                s = jnp.where(pl.program_id(1) >= lead_blocks - n, s, NEG_INF)
            s_blocks.append(s)
        outs.append(_softmax_pv(s_blocks, vs))
    o_ref[...] = jnp.where(first_head, outs[0], outs[1]).astype(o_ref.dtype)


def _rel_bias(rel_table, lq, lk, q_off):
    m = lq + lk
    pos = jnp.arange(m, dtype=I32)
    j_minus_i = jnp.where(pos < lk, pos, pos - m)
    diag = rel_table[:, jnp.clip(q_off - j_minus_i, -REL_CLIP, REL_CLIP) + REL_CLIP].astype(F32)
    h = rel_table.shape[0]
    tiled = jnp.broadcast_to(diag[:, None, :], (h, lq, m)).reshape(h, lq * m)[:, :lq * (m - 1)]
    return tiled.reshape(h, lq, m - 1)[:, :, :lk]


def _cb_prompt(qkv, rel_table, tq):
    s = qkv.shape[0]
    npair = CB_HEADS // 2
    lead = (CB_LEFT_CHUNKS * CHUNK) // tq
    nblk = lead + 1
    q_pos = lead * tq + jnp.arange(tq, dtype=I32)
    k_pos = jnp.arange(nblk * tq, dtype=I32)
    qc, kc = q_pos // CHUNK, k_pos // CHUNK
    band = (kc[None, :] >= qc[:, None] - CB_LEFT_CHUNKS) & (kc[None, :] <= qc[:, None])
    bias = jnp.where(band[None], _rel_bias(rel_table, tq, nblk * tq, lead * tq), NEG_INF)
    base = 3 * npair
    kv_specs = [pl.BlockSpec((tq, LANES), functools.partial(
        lambda p, i, back, col: (jnp.maximum(i - back, 0), col + p), back=lead - n, col=base + off))
        for off in (npair, 2 * npair) for n in range(nblk)]
    return pl.pallas_call(
        functools.partial(_cb_kernel, sizes=(tq,) * nblk, lead_blocks=lead),
        grid=(npair, s // tq),
        in_specs=[pl.BlockSpec((tq, LANES), lambda p, i: (i, base + p))] + kv_specs
        + [pl.BlockSpec((2, tq, nblk * tq), lambda p, i: (p, 0, 0))],
        out_specs=pl.BlockSpec((tq, LANES), lambda p, i: (i, p)),
        out_shape=jax.ShapeDtypeStruct((s, ATT_W), BF16),
        compiler_params=_cparams(("parallel", "arbitrary")),
        name="cb_prompt",
    )(qkv, *([qkv] * (2 * nblk)), bias)


def _cb_sample_kernel(q_ref, kn_ref, vn_ref, ck_ref, cv_ref, bias_ref, o_ref):
    q, kn, vn = q_ref[...], kn_ref[...], vn_ref[...]
    cb_past = ck_ref.shape[2]
    outs = []
    for h in range(CB_HEADS):
        qh = _head(q, h)
        s_old = _dot(qh, ck_ref[h].astype(BF16)) + bias_ref[h, :, :cb_past]
        s_new = _dot_nt(qh, _head(kn, h)) + bias_ref[h, :, cb_past:]
        m = jnp.maximum(jnp.max(s_old, axis=1, keepdims=True), jnp.max(s_new, axis=1, keepdims=True))
        p_old, p_new = jnp.exp(s_old - m), jnp.exp(s_new - m)
        denom = jnp.sum(p_old, axis=1, keepdims=True) + jnp.sum(p_new, axis=1, keepdims=True)
        pv = _dot_nt(p_old.astype(BF16), cv_ref[h].astype(BF16)) + _dot(p_new.astype(BF16), _head(vn, h))
        outs.append(pv / denom)
    o_ref[...] = jnp.concatenate(outs, axis=1).astype(o_ref.dtype)


def _cb_sample(qkv, cache_k, cache_v, rel_table, layer, lq):
    _, nb, nh, hd, cb_past = cache_k.shape
    bias = _rel_bias(rel_table, lq, cb_past + lq, cb_past)
    cache_spec = pl.BlockSpec((None, None, nh, hd, cb_past), lambda b: (layer, b, 0, 0, 0))
    qcol = 3
    return pl.pallas_call(
        _cb_sample_kernel,
        grid=(nb,),
        in_specs=[pl.BlockSpec((lq, ATT_W), lambda b: (b, qcol)),
                  pl.BlockSpec((lq, ATT_W), lambda b: (b, qcol + 1)),
                  pl.BlockSpec((lq, ATT_W), lambda b: (b, qcol + 2)),
                  cache_spec, cache_spec, _resident(bias)],
        out_specs=pl.BlockSpec((lq, ATT_W), lambda b: (b, 0)),
        out_shape=jax.ShapeDtypeStruct((nb * lq, ATT_W), BF16),
        compiler_params=_cparams(("parallel",)),
        name="cb_sample",
    )(qkv, qkv, qkv, cache_k, cache_v, bias)


def _memkv_kernel(mem_ref, g_ref, w_ref, k_ref, v_ref):
    h = _rms(mem_ref[...], g_ref[...]).astype(BF16)
    kv = _dot(h, w_ref[...])
    half = k_ref.shape[1]
    k_ref[...] = kv[:, :half]
    v_ref[...] = kv[:, half:]


def _memkv(mem, g, w_bf16):
    n = mem.shape[0]
    half = w_bf16.shape[1] // 2
    return pl.pallas_call(
        _memkv_kernel,
        out_shape=[jax.ShapeDtypeStruct((n, half), F32)] * 2,
        compiler_params=pltpu.CompilerParams(vmem_limit_bytes=VMEM_LIMIT),
        name="memkv",
    )(mem, g, w_bf16)


def _xattn_kernel(q_ref, k_ref, v_ref, o_ref, *, rows_by_head):
    q = q_ref[...]
    for h in range(MEM_HEADS):
        cols = slice(h * MEM_HEAD_DIM, (h + 1) * MEM_HEAD_DIM)
        if rows_by_head:
            rows = pl.ds(h, k_ref.shape[0] // MEM_HEADS, stride=MEM_HEADS)
            k, v = k_ref[rows, :].astype(BF16), v_ref[rows, :].astype(BF16)
        else:
            k, v = k_ref[:, cols].astype(BF16), v_ref[:, cols].astype(BF16)
        s = _dot_nt(q[:, cols], k) * MEM_SCALE
        o_ref[:, cols] = _softmax_pv([s], [v]).astype(o_ref.dtype)


def _xattn(qkv, mem_k, mem_v, tq, per_tile_kv):
    t = qkv.shape[0]
    qcol = (N_QKV_SEG - 1)
    if per_tile_kv:
        kv_spec = pl.BlockSpec((None,) + mem_k.shape[1:], lambda i: (i, 0, 0))
    else:
        kv_spec = pl.BlockSpec(mem_k.shape, lambda i: (0, 0))
    return pl.pallas_call(
        functools.partial(_xattn_kernel, rows_by_head=per_tile_kv),
        grid=(t // tq,),
        in_specs=[pl.BlockSpec((tq, ATT_W), lambda i: (i, qcol)), kv_spec, kv_spec],
        out_specs=pl.BlockSpec((tq, ATT_W), lambda i: (i, 0)),
        out_shape=jax.ShapeDtypeStruct((t, ATT_W), BF16),
        compiler_params=_cparams(("parallel",)),
        name="xattn",
    )(qkv, mem_k, mem_v)


def _merge_kernel(x_ref, osb_ref, ocb_ref, om_ref, g_ref, wsb_ref, wcb_ref, wm_ref, wo_ref,
                  gf_ref, wr_ref, br_ref, cnt0_ref,
                  y_ref, hf_ref, e_ref, gate_ref, rank_ref, cnt_ref, cnt_scr):
    i = pl.program_id(0)
    tm, d = x_ref.shape

    @pl.when(i == 0)
    def _():
        cnt_scr[...] = cnt0_ref[...].astype(F32)

    m = None
    for n, (o_ref, w_ref) in enumerate(((osb_ref, wsb_ref), (ocb_ref, wcb_ref), (om_ref, wm_ref))):
        term = g_ref[:, n * d:(n + 1) * d].astype(F32) * _dot(o_ref[...], w_ref[...])
        m = term if m is None else m + term
    y = x_ref[...] + _dot(m.astype(BF16), wo_ref[...])
    y_ref[...] = y
    hf = _rms(y, gf_ref[...])
    hf_ref[...] = hf
    logits = _dot(hf.astype(BF16), wr_ref[...]) + br_ref[...]

    ne = logits.shape[1]
    lane = lax.broadcasted_iota(I32, (tm, ne), 1)
    work = logits
    vals, sels, idxs = [], [], []
    for _ in range(TOP_K):
        mx = jnp.max(work, axis=1, keepdims=True)
        idx = jnp.min(jnp.where(work == mx, lane, ne), axis=1, keepdims=True)
        sel = lane == idx
        vals.append(mx)
        idxs.append(idx)
        sels.append(sel)
        work = jnp.where(sel, -jnp.inf, work)
    ex = [jnp.exp(v - vals[0]) for v in vals]
    denom = functools.reduce(jnp.add, ex)

    onehot = functools.reduce(jnp.add, [s.astype(F32) for s in sels])
    before = (lax.broadcasted_iota(I32, (tm, tm), 1) < lax.broadcasted_iota(I32, (tm, tm), 0)).astype(BF16)
    prior = _dot(before, onehot.astype(BF16)) + cnt_scr[...]
    for k in range(TOP_K):
        e_ref[:, k:k + 1] = idxs[k]
        gate_ref[:, k:k + 1] = ex[k] / denom
        rank_ref[:, k:k + 1] = jnp.sum(jnp.where(sels[k], prior, 0.0), axis=1, keepdims=True).astype(I32)
    cnt_scr[...] += jnp.sum(onehot, axis=0, keepdims=True)
    cnt_ref[...] = cnt_scr[...].astype(I32)


def _merge(x, o_sb, o_cb, o_m, gates, w_sb, w_cb, w_m, w_out, g_ffn, w_router, b_router, cnt0, tm):
    t, d = x.shape
    ne = w_router.shape[1]
    row = lambda i: (i, 0)
    return pl.pallas_call(
        _merge_kernel,
        grid=(t // tm,),
        in_specs=[pl.BlockSpec((tm, d), row)] + [pl.BlockSpec((tm, ATT_W), row)] * 3
        + [pl.BlockSpec((tm, gates.shape[1]), row)]
        + [_resident(a) for a in (w_sb, w_cb, w_m, w_out, g_ffn, w_router, b_router, cnt0)],
        out_specs=[pl.BlockSpec((tm, d), row), pl.BlockSpec((tm, d), row)]
        + [pl.BlockSpec((tm, TOP_K), row)] * 3 + [pl.BlockSpec((1, ne), lambda i: (0, 0))],
        out_shape=[jax.ShapeDtypeStruct((t, d), F32), jax.ShapeDtypeStruct((t, d), F32),
                   jax.ShapeDtypeStruct((t, TOP_K), I32), jax.ShapeDtypeStruct((t, TOP_K), F32),
                   jax.ShapeDtypeStruct((t, TOP_K), I32), jax.ShapeDtypeStruct((1, ne), I32)],
        scratch_shapes=[pltpu.VMEM((1, ne), F32)],
        compiler_params=_cparams(("arbitrary",)),
        name="merge_router",
    )(x, o_sb, o_cb, o_m, gates, w_sb, w_cb, w_m, w_out, g_ffn, w_router, b_router, cnt0)


def _row_copies(make_copy, n_rows):
    def start(r, carry):
        for k in range(TOP_K):
            make_copy(r, k).start(priority=k % 2)
        return carry

    def wait(r, carry):
        for k in range(TOP_K):
            make_copy(r, k).wait()
        return carry

    lax.fori_loop(0, n_rows, start, 0, unroll=ROW_UNROLL)
    lax.fori_loop(0, n_rows, wait, 0, unroll=ROW_UNROLL)


def _n_chunks(count):
    return lax.shift_right_logical(count + (EXPERT_ROWS - 1), EXPERT_ROWS.bit_length() - 1)


def _used_rows(start_ref, cnt_ref):
    last = cnt_ref.shape[0] - 1
    return start_ref[last] + _n_chunks(cnt_ref[last]) * EXPERT_ROWS


def _zero_chunks(zeros_ref, dst_hbm, sem, n, chunk_row):
    def copy(c):
        rows = pl.ds(pl.multiple_of(chunk_row(c), EXPERT_ROWS), EXPERT_ROWS)
        return pltpu.make_async_copy(zeros_ref, dst_hbm.at[rows], sem)

    def start(c, carry):
        copy(c).start()
        return carry

    def wait(c, carry):
        copy(c).wait()
        return carry

    lax.fori_loop(0, n, start, 0)
    lax.fori_loop(0, n, wait, 0)


def _zero_tail(zeros_ref, dst_hbm, sem, start_ref, cnt_ref):
    used = _used_rows(start_ref, cnt_ref)
    n_tail = lax.shift_right_logical(dst_hbm.shape[0] - used, EXPERT_ROWS.bit_length() - 1)
    _zero_chunks(zeros_ref, dst_hbm, sem, n_tail, lambda c: used + c * EXPERT_ROWS)


def _dispatch_kernel(dest_ref, start_ref, cnt_ref, *refs, tiles):
    hf_refs, (xs_ref, sem, zbuf, zsem) = refs[:len(tiles)], refs[len(tiles):]
    i = pl.program_id(0)
    tm = hf_refs[0].shape[0]

    @pl.when(i == 0)
    def _():
        zbuf[...] = jnp.zeros_like(zbuf)
        last_row = xs_ref.shape[0] - EXPERT_ROWS

        def last_chunk(e):
            return jnp.minimum(start_ref[e] + jnp.maximum(_n_chunks(cnt_ref[e]) - 1, 0) * EXPERT_ROWS, last_row)

        _zero_chunks(zbuf, xs_ref, zsem, cnt_ref.shape[0], last_chunk)
        _zero_tail(zbuf, xs_ref, zsem, start_ref, cnt_ref)

    first_tile = 0
    for hf_ref, n_tiles in zip(hf_refs, tiles):
        @pl.when(jnp.logical_and(i >= first_tile, i < first_tile + n_tiles))
        def _(hf_ref=hf_ref):
            def row_copy(r, k):
                dst = dest_ref[(i * tm + r) * TOP_K + k]
                return pltpu.make_async_copy(hf_ref.at[pl.ds(r, 1)], xs_ref.at[pl.ds(dst, 1)], sem)

            _row_copies(row_copy, tm)
        first_tile += n_tiles


def _dispatch(dest_flat, row_start, counts, hfs, rows, tm):
    d = hfs[0].shape[1]
    tiles = tuple(hf.shape[0] // tm for hf in hfs)
    firsts = np.concatenate([[0], np.cumsum(tiles)[:-1]])
    specs = [pl.BlockSpec((tm, d), functools.partial(
        lambda i, *_, first, n: (jnp.clip(i - first, 0, n - 1), 0), first=int(f), n=n))
        for f, n in zip(firsts, tiles)]
    return pl.pallas_call(
        functools.partial(_dispatch_kernel, tiles=tiles),
        grid_spec=pltpu.PrefetchScalarGridSpec(
            num_scalar_prefetch=3,
            grid=(sum(tiles),),
            in_specs=specs,
            out_specs=pl.BlockSpec(memory_space=pl.ANY),
            scratch_shapes=[pltpu.SemaphoreType.DMA, pltpu.VMEM((EXPERT_ROWS, d), hfs[0].dtype),
                            pltpu.SemaphoreType.DMA]),
        out_shape=jax.ShapeDtypeStruct((rows, d), hfs[0].dtype),
        compiler_params=_cparams(("arbitrary",)),
        name="moe_dispatch",
    )(dest_flat, row_start, counts, *hfs)


def _expert_kernel(start_ref, cnt_ref, xs_hbm, wup_ref, wd_ref, bg_ref, bu_ref, bd_ref, ys_hbm,
                   wg_scr, wu_scr, wd_scr, xbuf, ybuf, sem_in, sem_out):
    e = pl.program_id(0)
    row0 = start_ref[e]
    n = _n_chunks(cnt_ref[e])

    def rows(j):
        return pl.ds(pl.multiple_of(row0 + j * EXPERT_ROWS, EXPERT_ROWS), EXPERT_ROWS)

    def in_copy(j, slot):
        return pltpu.make_async_copy(xs_hbm.at[rows(j)], xbuf.at[slot], sem_in.at[slot])

    def out_copy(j, slot):
        return pltpu.make_async_copy(ybuf.at[slot], ys_hbm.at[rows(j)], sem_out.at[slot])

    @pl.when(n > 0)
    def _():
        in_copy(0, 0).start(priority=CHUNK_DMA_PRIORITY)
        grp = 2 * LANES
        src = lax.broadcasted_iota(I32, (grp, grp), 0)
        dst = lax.broadcasted_iota(I32, (grp, grp), 1)
        perm = (src == jnp.where(dst < LANES, 2 * dst, 2 * (dst - LANES) + 1)).astype(BF16)
        for c in range(wup_ref.shape[1] // grp):
            w = _dot(wup_ref[:, c * grp:(c + 1) * grp].astype(BF16), perm).astype(BF16)
            wg_scr[:, c * LANES:(c + 1) * LANES] = w[:, :LANES]
            wu_scr[:, c * LANES:(c + 1) * LANES] = w[:, LANES:]
        wd_scr[...] = wd_ref[...].astype(BF16)

        def chunk(j, carry):
            slot = lax.rem(j, 2)
            in_copy(j, slot).wait()

            @pl.when(j + 1 < n)
            def _():
                in_copy(j + 1, 1 - slot).start(priority=CHUNK_DMA_PRIORITY)

            @pl.when(j >= 2)
            def _():
                out_copy(j - 2, slot).wait()

            x = xbuf[slot].astype(BF16)
            g = jnp.minimum(_dot(x, wg_scr[...]) + bg_ref[...], SWIGLU_LIMIT)
            u = jnp.clip(_dot(x, wu_scr[...]) + bu_ref[...], -SWIGLU_LIMIT, SWIGLU_LIMIT)
            act = (u + 1.0) * (g * jax.nn.sigmoid(SWIGLU_ALPHA * g))
            ybuf[slot] = _dot(act.astype(BF16), wd_scr[...]) + bd_ref[...]
            out_copy(j, slot).start(priority=CHUNK_DMA_PRIORITY)
            return carry

        lax.fori_loop(0, n, chunk, 0)

        @pl.when(n >= 2)
        def _():
            out_copy(n - 2, lax.rem(n, 2)).wait()

        out_copy(n - 1, lax.rem(n - 1, 2)).wait()

    @pl.when(e == pl.num_programs(0) - 1)
    def _():
        ybuf[0] = jnp.zeros(ybuf.shape[1:], ybuf.dtype)
        _zero_tail(ybuf.at[0], ys_hbm, sem_out.at[0], start_ref, cnt_ref)


def _experts(row_start, counts, xs, w_up, w_d, b_g, b_u, b_d):
    rows, d = xs.shape
    ne, f = w_d.shape[0], w_d.shape[1]
    wspec = lambda a: pl.BlockSpec((None,) + a.shape[1:], lambda e, *_: (e, 0, 0))
    hbm = pl.BlockSpec(memory_space=pl.ANY)
    return pl.pallas_call(
        _expert_kernel,
        grid_spec=pltpu.PrefetchScalarGridSpec(
            num_scalar_prefetch=2,
            grid=(ne,),
            in_specs=[hbm] + [wspec(a) for a in (w_up, w_d, b_g, b_u, b_d)],
            out_specs=hbm,
            scratch_shapes=[pltpu.VMEM((d, f), BF16), pltpu.VMEM((d, f), BF16), pltpu.VMEM((f, d), BF16),
                            pltpu.VMEM((2, EXPERT_ROWS, d), xs.dtype), pltpu.VMEM((2, EXPERT_ROWS, d), F32),
                            pltpu.SemaphoreType.DMA((2,)), pltpu.SemaphoreType.DMA((2,))]),
        out_shape=jax.ShapeDtypeStruct((rows, d), F32),
        compiler_params=_cparams(("arbitrary",)),
        name="moe_experts",
    )(row_start, counts, xs, w_up, w_d, b_g, b_u, b_d)


def _combine_kernel(dest_ref, y_ref, gate_ref, gfin_ref, ys_ref, o_ref, buf, sem):
    i = pl.program_id(0)
    tm = y_ref.shape[0]

    def row_copy(r, k):
        src = dest_ref[(i * tm + r) * TOP_K + k]
        return pltpu.make_async_copy(ys_ref.at[pl.ds(src, 1)], buf.at[k, pl.ds(r, 1)], sem)

    _row_copies(row_copy, tm)
    moe =functools.reduce(jnp.add, [buf[k] * gate_ref[:, k:k + 1] for k in range(TOP_K)])
    o_ref[...] = _rms(y_ref[...] + moe, gfin_ref[...])


def _combine(dest_flat, y, gate, g_final, ys, tm):
    t, d = y.shape
    return pl.pallas_call(
        _combine_kernel,
        grid_spec=pltpu.PrefetchScalarGridSpec(
            num_scalar_prefetch=1,
            grid=(t // tm,),
            in_specs=[pl.BlockSpec((tm, d), lambda i, dest: (i, 0)),
                      pl.BlockSpec((tm, TOP_K), lambda i, dest: (i, 0)),
                      pl.BlockSpec((1, d), lambda i, dest: (0, 0)),
                      pl.BlockSpec(memory_space=pl.ANY)],
            out_specs=pl.BlockSpec((tm, d), lambda i, dest: (i, 0)),
            scratch_shapes=[pltpu.VMEM((TOP_K, tm, d), F32), pltpu.SemaphoreType.DMA]),
        out_shape=jax.ShapeDtypeStruct((t, d), F32),
        compiler_params=_cparams(("arbitrary",)),
        name="moe_combine",
    )(dest_flat, y, gate, g_final, ys)


def _common_tile(sizes, pref):
    t = pref
    while any(s % t for s in sizes):
        t //= 2
    return t


def _moe(groups, counts, w_up, w_d, b_g, b_u, b_d, g_final):
    ne = counts.shape[1]
    n_pairs = sum(g[0].shape[0] for g in groups) * TOP_K
    rows = (n_pairs + ne * (EXPERT_ROWS - 1) + EXPERT_ROWS - 1) // EXPERT_ROWS * EXPERT_ROWS
    counts = counts[0]
    chunks = (counts + EXPERT_ROWS - 1) // EXPERT_ROWS
    eid = jnp.arange(ne, dtype=I32)
    row_start = jnp.sum(jnp.where(eid[None, :] < eid[:, None], chunks[None, :], 0), axis=1) * EXPERT_ROWS
    dests = [(jnp.sum(jnp.where(top_e[..., None] == eid, row_start, 0), axis=-1) + rank).reshape(-1)
             for _, _, top_e, _, rank in groups]
    sizes = [g[0].shape[0] for g in groups]
    xs = _dispatch(jnp.concatenate(dests), row_start, counts, [g[1] for g in groups], rows,
                   _common_tile(sizes, DISPATCH_ROWS))
    ys = _experts(row_start, counts, xs, w_up, w_d, b_g, b_u, b_d)
    return [_combine(dest, y, gate, g_final, ys, _common_tile([y.shape[0]], COMBINE_ROWS))
            for (y, _, _, gate, _), dest in zip(groups, dests)]


def _tile(n, pref):
    return pref if n % pref == 0 else n


def _attn_block(x, attend, weights, cnt0, tm):
    g_attn, w_in, b_gate, w_sb, w_cb, w_m, w_out, g_ffn, w_router, b_router = weights
    qkv, ksb, vsb, kcb, vcb, gates = _inproj(x, g_attn, w_in, b_gate, tm)
    o_sb, o_cb, o_m = attend(qkv)
    routed = _merge(x, o_sb, o_cb, o_m, gates, w_sb, w_cb, w_m, w_out, g_ffn, w_router, b_router, cnt0, tm)
    return routed, (ksb, vsb, kcb, vcb)


def kernel(x_prompt, x_sample, mem_prompt, cache_sb_k, cache_sb_v, cache_cb_k, cache_cb_v, cache_mem_k, cache_mem_v, g_attn, w_in, b_gate, rel_table, g_mem, w_mem_kv, w_br_sb, w_br_cb, w_br_mem, w_out, g_ffn, w_router, b_router, w_up, b_up, w_down, b_down, g_final):
    depth = w_in.shape[0]
    assert depth == 1, "one layer: the per-group outputs below are the final residual streams"
    bp, seq, d = x_prompt.shape
    nb, lq, _ = x_sample.shape
    assert bp == 1
    past = cache_sb_k.shape[2]
    band_keep = min(CB_LEFT_CHUNKS * CHUNK, seq)
    l = 0
    row2 = lambda a: a.reshape(1, -1)
    weights = (row2(g_attn[l]), w_in[l].astype(BF16), row2(b_gate[l]),
               w_br_sb[l].astype(BF16), w_br_cb[l].astype(BF16), w_br_mem[l].astype(BF16),
               w_out[l].astype(BF16), row2(g_ffn[l]), w_router[l].astype(BF16), row2(b_router[l]))

    mk, mv = _memkv(mem_prompt[0], row2(g_mem[l]), w_mem_kv[l].astype(BF16))

    def attend_prompt(qkv):
        return (_sb_prompt(qkv, _tile(seq, 256)), _cb_prompt(qkv, rel_table[l], _tile(seq, 256)),
                _xattn(qkv, mk, mv, _tile(seq, 512), per_tile_kv=False))

    def attend_sample(qkv):
        flat = lambda c: c.reshape(c.shape[0], c.shape[1] * c.shape[2], c.shape[3])
        return (_sb_sample(qkv, _seq_minor(cache_sb_k), _seq_minor(cache_sb_v), l, lq, _tile(past, 256)),
                _cb_sample(qkv, _seq_minor(cache_cb_k), _seq_minor(cache_cb_v), rel_table[l], l, lq),
                _xattn(qkv, flat(cache_mem_k[l]), flat(cache_mem_v[l]), lq, per_tile_kv=True))

    no_tokens = jnp.zeros((1, w_router.shape[2]), I32)
    routed_p, (ksb_p, vsb_p, kcb_p, vcb_p) = _attn_block(
        x_prompt.reshape(seq, d), attend_prompt, weights, no_tokens, _tile(seq, 512))
    routed_s, (ksb_s, vsb_s, kcb_s, vcb_s) = _attn_block(
        x_sample.reshape(nb * lq, d), attend_sample, weights, routed_p[5], _tile(nb * lq, 512))
    yp, ys = _moe([routed_p[:5], routed_s[:5]], routed_s[5], w_up[l], w_down[l],
                  b_up[l][:, None, 0::2], b_up[l][:, None, 1::2], b_down[l][:, None, :], row2(g_final))

    heads = lambda a, b, n, h, hd: a.reshape(1, b, n, h, hd)
    return (yp.reshape(1, seq, d), ys.reshape(nb, lq, d),
            heads(ksb_p, 1, seq, SB_HEADS, HEAD_DIM), heads(vsb_p, 1, seq, SB_HEADS, HEAD_DIM),
            heads(kcb_p[seq - band_keep:], 1, band_keep, CB_HEADS, HEAD_DIM),
            heads(vcb_p[seq - band_keep:], 1, band_keep, CB_HEADS, HEAD_DIM),
            heads(mk, 1, N_MEM, MEM_HEADS, MEM_HEAD_DIM), heads(mv, 1, N_MEM, MEM_HEADS, MEM_HEAD_DIM),
            heads(ksb_s, nb, lq, SB_HEADS, HEAD_DIM), heads(vsb_s, nb, lq, SB_HEADS, HEAD_DIM),
            heads(kcb_s, nb, lq, CB_HEADS, HEAD_DIM), heads(vcb_s, nb, lq, CB_HEADS, HEAD_DIM))
```

```python
import functools

import jax
import jax.numpy as jnp
import numpy as np
from jax import lax
from jax.experimental import pallas as pl
from jax.experimental.pallas import tpu as pltpu

F32 = jnp.float32
BF16 = jnp.bfloat16
I32 = jnp.int32

CHUNK = 64
N_MEM = 256
SB_HEADS = 8
CB_HEADS = 8
HEAD_DIM = 64
MEM_HEADS = 4
MEM_HEAD_DIM = 128
CB_LEFT_CHUNKS = 8
REL_CLIP = 128
ATT_W = 512
N_QKV_SEG = 7
N_EXPERTS = 32
TOP_K = 4
SWIGLU_LIMIT = 7.0
SWIGLU_ALPHA = 1.702
RMS_EPS = 1e-6
NEG_INF = -1e30

LANES = 128
VMEM_LIMIT = 56 * 1024 * 1024
SB_UNDERFLOW = 104.0
LOG2_E = 1.4426950408889634

EXPERT_ROWS = 256
CHUNK_DMA_PRIORITY = 1
DISPATCH_ROWS = 256
COMBINE_ROWS = 512
SB_PAIRS_PER_STEP = 4
ROW_UNROLL = 8

QK_SCALE = HEAD_DIM ** -0.5
MEM_SCALE = MEM_HEAD_DIM ** -0.5


def _cparams(sem):
    return pltpu.CompilerParams(dimension_semantics=sem, vmem_limit_bytes=VMEM_LIMIT)


def _rms(x, g):
    return x * lax.rsqrt(jnp.mean(x * x, axis=-1, keepdims=True) + RMS_EPS) * g


def _dot(a, b):
    return jnp.dot(a, b, preferred_element_type=F32)


def _dot_nt(a, b):
    return lax.dot_general(a, b, (((1,), (1,)), ((), ())), preferred_element_type=F32)


def _inproj_kernel(x_ref, g_ref, w_ref, bg_ref, qkv_ref, ksb_ref, vsb_ref, kcb_ref, vcb_ref, gate_ref, h_scr):
    h_scr[...] = _rms(x_ref[...], g_ref[...]).astype(BF16)
    f32_out = {1: ksb_ref, 2: vsb_ref, 4: kcb_ref, 5: vcb_ref}
    for seg in range(w_ref.shape[1] // ATT_W):
        cols = slice(seg * ATT_W, (seg + 1) * ATT_W)
        acc = _dot(h_scr[...], w_ref[:, cols])
        if seg < N_QKV_SEG:
            qkv_ref[:, cols] = (acc * (QK_SCALE if seg in (0, 3) else 1.0)).astype(BF16)
            if seg in f32_out:
                f32_out[seg][...] = acc
        else:
            gcols = slice((seg - N_QKV_SEG) * ATT_W, (seg - N_QKV_SEG + 1) * ATT_W)
            gate_ref[:, gcols] = jax.nn.sigmoid(acc + bg_ref[:, gcols]).astype(BF16)


def _resident(a):
    return pl.BlockSpec(a.shape, lambda *_: (0,) * a.ndim, pipeline_mode=pl.Buffered(1))


def _inproj(x, g, w_bf16, b_gate, tm):
    t, d = x.shape
    n_gate = w_bf16.shape[1] - N_QKV_SEG * ATT_W
    row = lambda i: (i, 0)
    f32_spec = pl.BlockSpec((tm, ATT_W), row)
    return pl.pallas_call(
        _inproj_kernel,
        grid=(t // tm,),
        in_specs=[pl.BlockSpec((tm, d), row), _resident(g), _resident(w_bf16), _resident(b_gate)],
        out_specs=[pl.BlockSpec((tm, N_QKV_SEG * ATT_W), row), f32_spec, f32_spec, f32_spec, f32_spec,
                   pl.BlockSpec((tm, n_gate), row)],
        out_shape=[jax.ShapeDtypeStruct((t, N_QKV_SEG * ATT_W), BF16)]
        + [jax.ShapeDtypeStruct((t, ATT_W), F32)] * 4
        + [jax.ShapeDtypeStruct((t, n_gate), BF16)],
        scratch_shapes=[pltpu.VMEM((tm, d), BF16)],
        compiler_params=_cparams(("parallel",)),
        name="inproj",
    )(x, g, w_bf16, b_gate)


def _head(a, h):
    return a[:, h * HEAD_DIM:(h + 1) * HEAD_DIM]


def _head_select(shape):
    return lax.broadcasted_iota(I32, shape, 1) < HEAD_DIM


def _suffix_ones(n):
    return (lax.broadcasted_iota(I32, (n, n), 0) > lax.broadcasted_iota(I32, (n, n), 1)).astype(BF16)


def _sb_block(q, kb, vb, c, u, causal=None, valid=None, seq_minor=False):
    z2 = (_dot(q, kb) if seq_minor else _dot_nt(q, kb)) * LOG2_E
    sp = jnp.maximum(z2, 0.0) + jnp.log2(1.0 + jnp.exp2(-jnp.abs(z2)))
    keep = causal if valid is None else (valid if causal is None else jnp.logical_and(causal, valid))
    l1m = -sp if keep is None else jnp.where(keep, -sp, 0.0)
    hi = l1m.astype(BF16)
    lo = (l1m - hi.astype(F32)).astype(BF16)
    tail = _dot(hi, u) + _dot(lo, u)
    a = jnp.exp2((z2 - sp) + tail + c)
    if keep is not None:
        a = jnp.where(keep, a, 0.0)
    pv = _dot_nt(a.astype(BF16), vb) if seq_minor else _dot(a.astype(BF16), vb)
    return pv, c + jnp.sum(l1m, axis=1, keepdims=True)


def _any_live(cs):
    return (functools.reduce(jnp.maximum, [jnp.max(c) for c in cs]) > -SB_UNDERFLOW * LOG2_E).astype(I32)


def _sb_older_blocks(first_block, fetch, step, cs, accs):
    n = len(cs)

    def cond(carry):
        return jnp.logical_and(carry[0] >= 0, carry[1] > 0)

    def body(carry):
        b, cs, accs = carry[0], carry[2:2 + n], carry[2 + n:]
        kb, vb = fetch(b)
        res = [step(s, kb, vb, cs[s]) for s in range(n)]
        cs = [r[1] for r in res]
        return (b - 1, _any_live(cs), *cs, *[acc + r[0] for acc, r in zip(accs, res)])

    out = lax.while_loop(cond, body, (first_block, _any_live(cs), *cs, *accs))
    return out[2 + n:]


def _fetch_blocks(copies):
    for c in copies:
        c.start()
    for c in copies:
        c.wait()


def _sb_prompt_kernel(q_ref, kd_ref, vd_ref, kp_ref, vp_ref, qkv_hbm, o_ref, kbuf, vbuf, sem, *, kcol, vcol):
    g, i = pl.program_id(0), pl.program_id(1)
    tq, w = q_ref.shape
    npair = w // LANES
    pair_lanes = lambda a, p: a[:, p * LANES:(p + 1) * LANES]
    first_head = _head_select((tq, LANES))
    causal = lax.broadcasted_iota(I32, (tq, tq), 1) < lax.broadcasted_iota(I32, (tq, tq), 0)
    u = _suffix_ones(tq)
    qs, cs, accs = [], [], []
    for p in range(npair):
        q = pair_lanes(q_ref, p)
        kd, vd, kp, vp = (pair_lanes(r, p) for r in (kd_ref, vd_ref, kp_ref, vp_ref))
        for qa in (jnp.where(first_head, q, jnp.zeros_like(q)), jnp.where(first_head, jnp.zeros_like(q), q)):
            pv_d, c = _sb_block(qa, kd, vd, jnp.zeros((tq, 1), F32), u, causal=causal)
            pv_p, c = _sb_block(qa, kp, vp, c, u, valid=i > 0)
            qs.append(qa)
            cs.append(c)
            accs.append(pv_d + pv_p)

    def fetch(b):
        rows = pl.ds(pl.multiple_of(b * tq, tq), tq)
        slab = lambda col: pl.ds(pl.multiple_of(col + g * w, LANES), w)
        _fetch_blocks([pltpu.make_async_copy(qkv_hbm.at[rows, slab(kcol)], kbuf, sem.at[0]),
                       pltpu.make_async_copy(qkv_hbm.at[rows, slab(vcol)], vbuf, sem.at[1])])
        return kbuf, vbuf

    def step(s, kb, vb, c):
        return _sb_block(qs[s], pair_lanes(kb, s // 2), pair_lanes(vb, s // 2), c, u)

    accs = _sb_older_blocks(i - 2, fetch, step, cs, accs)
    for p in range(npair):
        o_ref[:, p * LANES:(p + 1) * LANES] = jnp.where(first_head, accs[2 * p], accs[2 * p + 1]).astype(o_ref.dtype)


def _sb_prompt(qkv, tq):
    s = qkv.shape[0]
    w = SB_PAIRS_PER_STEP * LANES
    groups = ATT_W // w
    kcol, vcol = groups, 2 * groups
    prev = lambda col: pl.BlockSpec((tq, w), lambda g, i: (jnp.maximum(i - 1, 0), col + g))
    return pl.pallas_call(
        functools.partial(_sb_prompt_kernel, kcol=kcol * w, vcol=vcol * w),
        grid=(groups, s // tq),
        in_specs=[pl.BlockSpec((tq, w), lambda g, i: (i, g)),
                  pl.BlockSpec((tq, w), lambda g, i: (i, kcol + g)),
                  pl.BlockSpec((tq, w), lambda g, i: (i, vcol + g)),
                  prev(kcol), prev(vcol),
                  pl.BlockSpec(memory_space=pl.ANY)],
        out_specs=pl.BlockSpec((tq, w), lambda g, i: (i, g)),
        out_shape=jax.ShapeDtypeStruct((s, ATT_W), BF16),
        scratch_shapes=[pltpu.VMEM((tq, w), BF16), pltpu.VMEM((tq, w), BF16),
                        pltpu.SemaphoreType.DMA((2,))],
        compiler_params=_cparams(("arbitrary", "arbitrary")),
        name="sb_prompt",
    )(qkv, qkv, qkv, qkv, qkv, qkv)


def _sb_sample_kernel(q_ref, kn_ref, vn_ref, kp_ref, vp_ref, ck_hbm, cv_hbm, o_ref, kbuf, vbuf, sem, *, layer):
    batch = pl.program_id(0)
    lq = q_ref.shape[0]
    bk = kp_ref.shape[2]
    q, kn, vn = q_ref[...], kn_ref[...], vn_ref[...]
    causal = lax.broadcasted_iota(I32, (lq, lq), 1) < lax.broadcasted_iota(I32, (lq, lq), 0)
    u_new, u_old = _suffix_ones(lq), _suffix_ones(bk)
    qs = [_head(q, h) for h in range(SB_HEADS)]
    cs, accs = [], []
    for h in range(SB_HEADS):
        pv_n, c = _sb_block(qs[h], _head(kn, h), _head(vn, h), jnp.zeros((lq, 1), F32), u_new, causal=causal)
        pv_p, c = _sb_block(qs[h], kp_ref[h].astype(BF16), vp_ref[h].astype(BF16), c, u_old, seq_minor=True)
        cs.append(c)
        accs.append(pv_n + pv_p)

    def fetch(b):
        cols = pl.ds(pl.multiple_of(b * bk, bk), bk)
        _fetch_blocks([pltpu.make_async_copy(ck_hbm.at[layer, batch, :, :, cols], kbuf, sem.at[0]),
                       pltpu.make_async_copy(cv_hbm.at[layer, batch, :, :, cols], vbuf, sem.at[1])])
        return kbuf, vbuf

    def step(h, kb, vb, c):
        return _sb_block(qs[h], kb[h].astype(BF16), vb[h].astype(BF16), c, u_old, seq_minor=True)

    n_old = ck_hbm.shape[4] // bk
    accs = _sb_older_blocks(n_old - 2, fetch, step, cs, accs)
    o_ref[...] = jnp.concatenate(accs, axis=1).astype(o_ref.dtype)


def _seq_minor(cache):
    return jnp.transpose(cache, (0, 1, 3, 4, 2))


def _sb_sample(qkv, cache_k, cache_v, layer, lq, bk):
    _, nb, nh, hd, past = cache_k.shape
    assert past % bk == 0
    last = past // bk - 1
    newest = pl.BlockSpec((None, None, nh, hd, bk), lambda b: (layer, b, 0, 0, last))
    return pl.pallas_call(
        functools.partial(_sb_sample_kernel, layer=layer),
        grid=(nb,),
        in_specs=[pl.BlockSpec((lq, ATT_W), lambda b: (b, 0)),
                  pl.BlockSpec((lq, ATT_W), lambda b: (b, 1)),
                  pl.BlockSpec((lq, ATT_W), lambda b: (b, 2)),
                  newest, newest,
                  pl.BlockSpec(memory_space=pl.ANY), pl.BlockSpec(memory_space=pl.ANY)],
        out_specs=pl.BlockSpec((lq, ATT_W), lambda b: (b, 0)),
        out_shape=jax.ShapeDtypeStruct((nb * lq, ATT_W), BF16),
        scratch_shapes=[pltpu.VMEM((nh, hd, bk), cache_k.dtype), pltpu.VMEM((nh, hd, bk), cache_v.dtype),
                        pltpu.SemaphoreType.DMA((2,))],
        compiler_params=_cparams(("arbitrary",)),
        name="sb_sample",
    )(qkv, qkv, qkv, cache_k, cache_v, cache_k, cache_v)


def _softmax_pv(s_blocks, v_blocks):
    m = jnp.max(functools.reduce(jnp.maximum, s_blocks), axis=1, keepdims=True)
    p_blocks = [jnp.exp(s - m) for s in s_blocks]
    denom = jnp.sum(functools.reduce(jnp.add, p_blocks), axis=1, keepdims=True)
    pv = functools.reduce(jnp.add, [_dot(p.astype(BF16), v) for p, v in zip(p_blocks, v_blocks)])
    return pv / denom


def _cb_kernel(q_ref, *refs, nblk, lead_blocks):
    k_refs, v_refs = refs[:nblk], refs[nblk:2 * nblk]
    bias_ref, o_ref = refs[2 * nblk], refs[2 * nblk + 1]
    tq = q_ref.shape[0]
    first_head = _head_select((tq, LANES))
    for pair in range(CB_HEADS // 2):
        lanes = slice(pair * LANES, (pair + 1) * LANES)
        q = q_ref[:, lanes]
        ks = [r[:, lanes] for r in k_refs]
        vs = [r[:, lanes] for r in v_refs]
        outs = []
        for head in range(2):
            qa = jnp.where(first_head if head == 0 else ~first_head, q, jnp.zeros_like(q))
            s_blocks = []
            for n in range(nblk):
                b = bias_ref[2 * pair + head, :, n * tq:(n + 1) * tq]
                s = _dot_nt(qa, ks[n]) + b
                if n < lead_blocks:
                    s = jnp.where(pl.program_id(0) >= lead_blocks - n, s, NEG_INF)
                s_blocks.append(s)
            outs.append(_softmax_pv(s_blocks, vs))
        o_ref[:, lanes] = jnp.where(first_head, outs[0], outs[1]).astype(o_ref.dtype)


def _rel_bias(rel_table, lq, lk, q_off):
    m = lq + lk
    pos = jnp.arange(m, dtype=I32)
    j_minus_i = jnp.where(pos < lk, pos, pos - m)
    diag = rel_table[:, jnp.clip(q_off - j_minus_i, -REL_CLIP, REL_CLIP) + REL_CLIP].astype(F32)
    h = rel_table.shape[0]
    tiled = jnp.broadcast_to(diag[:, None, :], (h, lq, m)).reshape(h, lq * m)[:, :lq * (m - 1)]
    return tiled.reshape(h, lq, m - 1)[:, :, :lk]


def _cb_prompt(qkv, rel_table, tq):
    s = qkv.shape[0]
    lead = (CB_LEFT_CHUNKS * CHUNK) // tq
    nblk = lead + 1
    q_pos = lead * tq + jnp.arange(tq, dtype=I32)
    k_pos = jnp.arange(nblk * tq, dtype=I32)
    qc, kc = q_pos // CHUNK, k_pos // CHUNK
    band = (kc[None, :] >= qc[:, None] - CB_LEFT_CHUNKS) & (kc[None, :] <= qc[:, None])
    bias = jnp.where(band[None], _rel_bias(rel_table, tq, nblk * tq, lead * tq), NEG_INF)
    qcol = 3
    kv_specs = [pl.BlockSpec((tq, ATT_W), functools.partial(
        lambda i, back, col: (jnp.maximum(i - back, 0), col), back=lead - n, col=col))
        for col in (qcol + 1, qcol + 2) for n in range(nblk)]
    return pl.pallas_call(
        functools.partial(_cb_kernel, nblk=nblk, lead_blocks=lead),
        grid=(s // tq,),
        in_specs=[pl.BlockSpec((tq, ATT_W), lambda i: (i, qcol))] + kv_specs + [_resident(bias)],
        out_specs=pl.BlockSpec((tq, ATT_W), lambda i: (i, 0)),
        out_shape=jax.ShapeDtypeStruct((s, ATT_W), BF16),
        compiler_params=_cparams(("parallel",)),
        name="cb_prompt",
    )(qkv, *([qkv] * (2 * nblk)), bias)


def _cb_sample_kernel(q_ref, kn_ref, vn_ref, ck_ref, cv_ref, bias_ref, o_ref):
    q, kn, vn = q_ref[...], kn_ref[...], vn_ref[...]
    cb_past = ck_ref.shape[2]
    outs = []
    for h in range(CB_HEADS):
        qh = _head(q, h)
        s_old = _dot(qh, ck_ref[h].astype(BF16)) + bias_ref[h, :, :cb_past]
        s_new = _dot_nt(qh, _head(kn, h)) + bias_ref[h, :, cb_past:]
        m = jnp.maximum(jnp.max(s_old, axis=1, keepdims=True), jnp.max(s_new, axis=1, keepdims=True))
        p_old, p_new = jnp.exp(s_old - m), jnp.exp(s_new - m)
        denom = jnp.sum(p_old, axis=1, keepdims=True) + jnp.sum(p_new, axis=1, keepdims=True)
        pv = _dot_nt(p_old.astype(BF16), cv_ref[h].astype(BF16)) + _dot(p_new.astype(BF16), _head(vn, h))
        outs.append(pv / denom)
    o_ref[...] = jnp.concatenate(outs, axis=1).astype(o_ref.dtype)


def _cb_sample(qkv, cache_k, cache_v, rel_table, layer, lq):
    _, nb, nh, hd, cb_past = cache_k.shape
    bias = _rel_bias(rel_table, lq, cb_past + lq, cb_past)
    cache_spec = pl.BlockSpec((None, None, nh, hd, cb_past), lambda b: (layer, b, 0, 0, 0))
    qcol = 3
    return pl.pallas_call(
        _cb_sample_kernel,
        grid=(nb,),
        in_specs=[pl.BlockSpec((lq, ATT_W), lambda b: (b, qcol)),
                  pl.BlockSpec((lq, ATT_W), lambda b: (b, qcol + 1)),
                  pl.BlockSpec((lq, ATT_W), lambda b: (b, qcol + 2)),
                  cache_spec, cache_spec, _resident(bias)],
        out_specs=pl.BlockSpec((lq, ATT_W), lambda b: (b, 0)),
        out_shape=jax.ShapeDtypeStruct((nb * lq, ATT_W), BF16),
        compiler_params=_cparams(("parallel",)),
        name="cb_sample",
    )(qkv, qkv, qkv, cache_k, cache_v, bias)


def _memkv_kernel(mem_ref, g_ref, w_ref, k_ref, v_ref):
    h = _rms(mem_ref[...], g_ref[...]).astype(BF16)
    kv = _dot(h, w_ref[...])
    half = k_ref.shape[1]
    k_ref[...] = kv[:, :half]
    v_ref[...] = kv[:, half:]


def _memkv(mem, g, w_bf16):
    n = mem.shape[0]
    half = w_bf16.shape[1] // 2
    return pl.pallas_call(
        _memkv_kernel,
        out_shape=[jax.ShapeDtypeStruct((n, half), F32)] * 2,
        compiler_params=pltpu.CompilerParams(vmem_limit_bytes=VMEM_LIMIT),
        name="memkv",
    )(mem, g, w_bf16)


def _xattn_kernel(q_ref, k_ref, v_ref, o_ref, *, rows_by_head):
    q = q_ref[...]
    for h in range(MEM_HEADS):
        cols = slice(h * MEM_HEAD_DIM, (h + 1) * MEM_HEAD_DIM)
        if rows_by_head:
            rows = pl.ds(h, k_ref.shape[0] // MEM_HEADS, stride=MEM_HEADS)
            k, v = k_ref[rows, :].astype(BF16), v_ref[rows, :].astype(BF16)
        else:
            k, v = k_ref[:, cols].astype(BF16), v_ref[:, cols].astype(BF16)
        s = _dot_nt(q[:, cols], k) * MEM_SCALE
        o_ref[:, cols] = _softmax_pv([s], [v]).astype(o_ref.dtype)


def _xattn(qkv, mem_k, mem_v, tq, per_tile_kv):
    t = qkv.shape[0]
    qcol = (N_QKV_SEG - 1)
    if per_tile_kv:
        kv_spec = pl.BlockSpec((None,) + mem_k.shape[1:], lambda i: (i, 0, 0))
    else:
        kv_spec = pl.BlockSpec(mem_k.shape, lambda i: (0, 0))
    return pl.pallas_call(
        functools.partial(_xattn_kernel, rows_by_head=per_tile_kv),
        grid=(t // tq,),
        in_specs=[pl.BlockSpec((tq, ATT_W), lambda i: (i, qcol)), kv_spec, kv_spec],
        out_specs=pl.BlockSpec((tq, ATT_W), lambda i: (i, 0)),
        out_shape=jax.ShapeDtypeStruct((t, ATT_W), BF16),
        compiler_params=_cparams(("parallel",)),
        name="xattn",
    )(qkv, mem_k, mem_v)


def _merge_kernel(x_ref, osb_ref, ocb_ref, om_ref, g_ref, wsb_ref, wcb_ref, wm_ref, wo_ref,
                  gf_ref, wr_ref, br_ref, cnt0_ref,
                  y_ref, hf_ref, e_ref, gate_ref, rank_ref, cnt_ref, cnt_scr):
    i = pl.program_id(0)
    tm, d = x_ref.shape

    @pl.when(i == 0)
    def _():
        cnt_scr[...] = cnt0_ref[...].astype(F32)

    m = None
    for n, (o_ref, w_ref) in enumerate(((osb_ref, wsb_ref), (ocb_ref, wcb_ref), (om_ref, wm_ref))):
        term = g_ref[:, n * d:(n + 1) * d].astype(F32) * _dot(o_ref[...], w_ref[...])
        m = term if m is None else m + term
    y = x_ref[...] + _dot(m.astype(BF16), wo_ref[...])
    y_ref[...] = y
    hf = _rms(y, gf_ref[...])
    hf_ref[...] = hf
    logits = _dot(hf.astype(BF16), wr_ref[...]) + br_ref[...]

    ne = logits.shape[1]
    lane = lax.broadcasted_iota(I32, (tm, ne), 1)
    work = logits
    vals, sels, idxs = [], [], []
    for _ in range(TOP_K):
        mx = jnp.max(work, axis=1, keepdims=True)
        idx = jnp.min(jnp.where(work == mx, lane, ne), axis=1, keepdims=True)
        sel = lane == idx
        vals.append(mx)
        idxs.append(idx)
        sels.append(sel)
        work = jnp.where(sel, -jnp.inf, work)
    ex = [jnp.exp(v - vals[0]) for v in vals]
    denom = functools.reduce(jnp.add, ex)

    onehot = functools.reduce(jnp.add, [s.astype(F32) for s in sels])
    before = (lax.broadcasted_iota(I32, (tm, tm), 1) < lax.broadcasted_iota(I32, (tm, tm), 0)).astype(BF16)
    prior = _dot(before, onehot.astype(BF16)) + cnt_scr[...]
    for k in range(TOP_K):
        e_ref[:, k:k + 1] = idxs[k]
        gate_ref[:, k:k + 1] = ex[k] / denom
        rank_ref[:, k:k + 1] = jnp.sum(jnp.where(sels[k], prior, 0.0), axis=1, keepdims=True).astype(I32)
    cnt_scr[...] += jnp.sum(onehot, axis=0, keepdims=True)
    cnt_ref[...] = cnt_scr[...].astype(I32)


def _merge(x, o_sb, o_cb, o_m, gates, w_sb, w_cb, w_m, w_out, g_ffn, w_router, b_router, cnt0, tm):
    t, d = x.shape
    ne = w_router.shape[1]
    row = lambda i: (i, 0)
    return pl.pallas_call(
        _merge_kernel,
        grid=(t // tm,),
        in_specs=[pl.BlockSpec((tm, d), row)] + [pl.BlockSpec((tm, ATT_W), row)] * 3
        + [pl.BlockSpec((tm, gates.shape[1]), row)]
        + [_resident(a) for a in (w_sb, w_cb, w_m, w_out, g_ffn, w_router, b_router, cnt0)],
        out_specs=[pl.BlockSpec((tm, d), row), pl.BlockSpec((tm, d), row)]
        + [pl.BlockSpec((tm, TOP_K), row)] * 3 + [pl.BlockSpec((1, ne), lambda i: (0, 0))],
        out_shape=[jax.ShapeDtypeStruct((t, d), F32), jax.ShapeDtypeStruct((t, d), F32),
                   jax.ShapeDtypeStruct((t, TOP_K), I32), jax.ShapeDtypeStruct((t, TOP_K), F32),
                   jax.ShapeDtypeStruct((t, TOP_K), I32), jax.ShapeDtypeStruct((1, ne), I32)],
        scratch_shapes=[pltpu.VMEM((1, ne), F32)],
        compiler_params=_cparams(("arbitrary",)),
        name="merge_router",
    )(x, o_sb, o_cb, o_m, gates, w_sb, w_cb, w_m, w_out, g_ffn, w_router, b_router, cnt0)


def _row_copies(make_copy, n_rows):
    def start(r, carry):
        for k in range(TOP_K):
            make_copy(r, k).start(priority=k % 2)
        return carry

    def wait(r, carry):
        for k in range(TOP_K):
            make_copy(r, k).wait()
        return carry

    lax.fori_loop(0, n_rows, start, 0, unroll=ROW_UNROLL)
    lax.fori_loop(0, n_rows, wait, 0, unroll=ROW_UNROLL)


def _n_chunks(count):
    return lax.shift_right_logical(count + (EXPERT_ROWS - 1), EXPERT_ROWS.bit_length() - 1)


def _used_rows(start_ref, cnt_ref):
    last = cnt_ref.shape[0] - 1
    return start_ref[last] + _n_chunks(cnt_ref[last]) * EXPERT_ROWS


def _zero_chunks(zeros_ref, dst_hbm, sem, n, chunk_row):
    def copy(c):
        rows = pl.ds(pl.multiple_of(chunk_row(c), EXPERT_ROWS), EXPERT_ROWS)
        return pltpu.make_async_copy(zeros_ref, dst_hbm.at[rows], sem)

    def start(c, carry):
        copy(c).start()
        return carry

    def wait(c, carry):
        copy(c).wait()
        return carry

    lax.fori_loop(0, n, start, 0)
    lax.fori_loop(0, n, wait, 0)


def _zero_tail(zeros_ref, dst_hbm, sem, start_ref, cnt_ref):
    used = _used_rows(start_ref, cnt_ref)
    n_tail = lax.shift_right_logical(dst_hbm.shape[0] - used, EXPERT_ROWS.bit_length() - 1)
    _zero_chunks(zeros_ref, dst_hbm, sem, n_tail, lambda c: used + c * EXPERT_ROWS)


def _dispatch_kernel(dest_ref, start_ref, cnt_ref, *refs, tiles):
    hf_refs, (xs_ref, sem, zbuf, zsem) = refs[:len(tiles)], refs[len(tiles):]
    i = pl.program_id(0)
    tm = hf_refs[0].shape[0]

    @pl.when(i == 0)
    def _():
        zbuf[...] = jnp.zeros_like(zbuf)
        last_row = xs_ref.shape[0] - EXPERT_ROWS

        def last_chunk(e):
            return jnp.minimum(start_ref[e] + jnp.maximum(_n_chunks(cnt_ref[e]) - 1, 0) * EXPERT_ROWS, last_row)

        _zero_chunks(zbuf, xs_ref, zsem, cnt_ref.shape[0], last_chunk)
        _zero_tail(zbuf, xs_ref, zsem, start_ref, cnt_ref)

    first_tile = 0
    for hf_ref, n_tiles in zip(hf_refs, tiles):
        @pl.when(jnp.logical_and(i >= first_tile, i < first_tile + n_tiles))
        def _(hf_ref=hf_ref):
            def row_copy(r, k):
                dst = dest_ref[(i * tm + r) * TOP_K + k]
                return pltpu.make_async_copy(hf_ref.at[pl.ds(r, 1)], xs_ref.at[pl.ds(dst, 1)], sem)

            _row_copies(row_copy, tm)
        first_tile += n_tiles


def _dispatch(dest_flat, row_start, counts, hfs, rows, tm):
    d = hfs[0].shape[1]
    tiles = tuple(hf.shape[0] // tm for hf in hfs)
    firsts = np.concatenate([[0], np.cumsum(tiles)[:-1]])
    specs = [pl.BlockSpec((tm, d), functools.partial(
        lambda i, *_, first, n: (jnp.clip(i - first, 0, n - 1), 0), first=int(f), n=n))
        for f, n in zip(firsts, tiles)]
    return pl.pallas_call(
        functools.partial(_dispatch_kernel, tiles=tiles),
        grid_spec=pltpu.PrefetchScalarGridSpec(
            num_scalar_prefetch=3,
            grid=(sum(tiles),),
            in_specs=specs,
            out_specs=pl.BlockSpec(memory_space=pl.ANY),
            scratch_shapes=[pltpu.SemaphoreType.DMA, pltpu.VMEM((EXPERT_ROWS, d), hfs[0].dtype),
                            pltpu.SemaphoreType.DMA]),
        out_shape=jax.ShapeDtypeStruct((rows, d), hfs[0].dtype),
        compiler_params=_cparams(("arbitrary",)),
        name="moe_dispatch",
    )(dest_flat, row_start, counts, *hfs)


def _expert_kernel(start_ref, cnt_ref, xs_hbm, wup_ref, wd_ref, bg_ref, bu_ref, bd_ref, ys_hbm,
                   wg_scr, wu_scr, wd_scr, xbuf, ybuf, sem_in, sem_out):
    e = pl.program_id(0)
    row0 = start_ref[e]
    n = _n_chunks(cnt_ref[e])

    def rows(j):
        return pl.ds(pl.multiple_of(row0 + j * EXPERT_ROWS, EXPERT_ROWS), EXPERT_ROWS)

    def in_copy(j, slot):
        return pltpu.make_async_copy(xs_hbm.at[rows(j)], xbuf.at[slot], sem_in.at[slot])

    def out_copy(j, slot):
        return pltpu.make_async_copy(ybuf.at[slot], ys_hbm.at[rows(j)], sem_out.at[slot])

    @pl.when(n > 0)
    def _():
        in_copy(0, 0).start(priority=CHUNK_DMA_PRIORITY)
        grp = 2 * LANES
        src = lax.broadcasted_iota(I32, (grp, grp), 0)
        dst = lax.broadcasted_iota(I32, (grp, grp), 1)
        perm = (src == jnp.where(dst < LANES, 2 * dst, 2 * (dst - LANES) + 1)).astype(BF16)
        for c in range(wup_ref.shape[1] // grp):
            w = _dot(wup_ref[:, c * grp:(c + 1) * grp].astype(BF16), perm).astype(BF16)
            wg_scr[:, c * LANES:(c + 1) * LANES] = w[:, :LANES]
            wu_scr[:, c * LANES:(c + 1) * LANES] = w[:, LANES:]
        wd_scr[...] = wd_ref[...].astype(BF16)

        def chunk(j, carry):
            slot = lax.rem(j, 2)
            in_copy(j, slot).wait()

            @pl.when(j + 1 < n)
            def _():
                in_copy(j + 1, 1 - slot).start(priority=CHUNK_DMA_PRIORITY)

            @pl.when(j >= 2)
            def _():
                out_copy(j - 2, slot).wait()

            x = xbuf[slot].astype(BF16)
            g = jnp.minimum(_dot(x, wg_scr[...]) + bg_ref[...], SWIGLU_LIMIT)
            u = jnp.clip(_dot(x, wu_scr[...]) + bu_ref[...], -SWIGLU_LIMIT, SWIGLU_LIMIT)
            act = (u + 1.0) * (g * jax.nn.sigmoid(SWIGLU_ALPHA * g))
            ybuf[slot] = _dot(act.astype(BF16), wd_scr[...]) + bd_ref[...]
            out_copy(j, slot).start(priority=CHUNK_DMA_PRIORITY)
            return carry

        lax.fori_loop(0, n, chunk, 0)

        @pl.when(n >= 2)
        def _():
            out_copy(n - 2, lax.rem(n, 2)).wait()

        out_copy(n - 1, lax.rem(n - 1, 2)).wait()

    @pl.when(e == pl.num_programs(0) - 1)
    def _():
        ybuf[0] = jnp.zeros(ybuf.shape[1:], ybuf.dtype)
        _zero_tail(ybuf.at[0], ys_hbm, sem_out.at[0], start_ref, cnt_ref)


def _experts(row_start, counts, xs, w_up, w_d, b_g, b_u, b_d):
    rows, d = xs.shape
    ne, f = w_d.shape[0], w_d.shape[1]
    wspec = lambda a: pl.BlockSpec((None,) + a.shape[1:], lambda e, *_: (e, 0, 0))
    hbm = pl.BlockSpec(memory_space=pl.ANY)
    return pl.pallas_call(
        _expert_kernel,
        grid_spec=pltpu.PrefetchScalarGridSpec(
            num_scalar_prefetch=2,
            grid=(ne,),
            in_specs=[hbm] + [wspec(a) for a in (w_up, w_d, b_g, b_u, b_d)],
            out_specs=hbm,
            scratch_shapes=[pltpu.VMEM((d, f), BF16), pltpu.VMEM((d, f), BF16), pltpu.VMEM((f, d), BF16),
                            pltpu.VMEM((2, EXPERT_ROWS, d), xs.dtype), pltpu.VMEM((2, EXPERT_ROWS, d), F32),
                            pltpu.SemaphoreType.DMA((2,)), pltpu.SemaphoreType.DMA((2,))]),
        out_shape=jax.ShapeDtypeStruct((rows, d), F32),
        compiler_params=_cparams(("arbitrary",)),
        name="moe_experts",
    )(row_start, counts, xs, w_up, w_d, b_g, b_u, b_d)


def _combine_kernel(dest_ref, y_ref, gate_ref, gfin_ref, ys_ref, o_ref, buf, sem):
    i = pl.program_id(0)
    tm = y_ref.shape[0]

    def row_copy(r, k):
        src = dest_ref[(i * tm + r) * TOP_K + k]
        return pltpu.make_async_copy(ys_ref.at[pl.ds(src, 1)], buf.at[k, pl.ds(r, 1)], sem)

    _row_copies(row_copy, tm)
    moe =functools.reduce(jnp.add, [buf[k] * gate_ref[:, k:k + 1] for k in range(TOP_K)])
    o_ref[...] = _rms(y_ref[...] + moe, gfin_ref[...])


def _combine(dest_flat, y, gate, g_final, ys, tm):
    t, d = y.shape
    return pl.pallas_call(
        _combine_kernel,
        grid_spec=pltpu.PrefetchScalarGridSpec(
            num_scalar_prefetch=1,
            grid=(t // tm,),
            in_specs=[pl.BlockSpec((tm, d), lambda i, dest: (i, 0)),
                      pl.BlockSpec((tm, TOP_K), lambda i, dest: (i, 0)),
                      pl.BlockSpec((1, d), lambda i, dest: (0, 0)),
                      pl.BlockSpec(memory_space=pl.ANY)],
            out_specs=pl.BlockSpec((tm, d), lambda i, dest: (i, 0)),
            scratch_shapes=[pltpu.VMEM((TOP_K, tm, d), F32), pltpu.SemaphoreType.DMA]),
        out_shape=jax.ShapeDtypeStruct((t, d), F32),
        compiler_params=_cparams(("arbitrary",)),
        name="moe_combine",
    )(dest_flat, y, gate, g_final, ys)


def _common_tile(sizes, pref):
    t = pref
    while any(s % t for s in sizes):
        t //= 2
    return t


def _moe(groups, counts, w_up, w_d, b_g, b_u, b_d, g_final):
    ne = counts.shape[1]
    n_pairs = sum(g[0].shape[0] for g in groups) * TOP_K
    rows = (n_pairs + ne * (EXPERT_ROWS - 1) + EXPERT_ROWS - 1) // EXPERT_ROWS * EXPERT_ROWS
    counts = counts[0]
    chunks = (counts + EXPERT_ROWS - 1) // EXPERT_ROWS
    eid = jnp.arange(ne, dtype=I32)
    row_start = jnp.sum(jnp.where(eid[None, :] < eid[:, None], chunks[None, :], 0), axis=1) * EXPERT_ROWS
    dests = [(jnp.sum(jnp.where(top_e[..., None] == eid, row_start, 0), axis=-1) + rank).reshape(-1)
             for _, _, top_e, _, rank in groups]
    sizes = [g[0].shape[0] for g in groups]
    xs = _dispatch(jnp.concatenate(dests), row_start, counts, [g[1] for g in groups], rows,
                   _common_tile(sizes, DISPATCH_ROWS))
    ys = _experts(row_start, counts, xs, w_up, w_d, b_g, b_u, b_d)
    return [_combine(dest, y, gate, g_final, ys, _common_tile([y.shape[0]], COMBINE_ROWS))
            for (y, _, _, gate, _), dest in zip(groups, dests)]


def _tile(n, pref):
    return pref if n % pref == 0 else n


def _attn_block(x, attend, weights, cnt0, tm):
    g_attn, w_in, b_gate, w_sb, w_cb, w_m, w_out, g_ffn, w_router, b_router = weights
    qkv, ksb, vsb, kcb, vcb, gates = _inproj(x, g_attn, w_in, b_gate, tm)
    o_sb, o_cb, o_m = attend(qkv)
    routed = _merge(x, o_sb, o_cb, o_m, gates, w_sb, w_cb, w_m, w_out, g_ffn, w_router, b_router, cnt0, tm)
    return routed, (ksb, vsb, kcb, vcb)


def kernel(x_prompt, x_sample, mem_prompt, cache_sb_k, cache_sb_v, cache_cb_k, cache_cb_v, cache_mem_k, cache_mem_v, g_attn, w_in, b_gate, rel_table, g_mem, w_mem_kv, w_br_sb, w_br_cb, w_br_mem, w_out, g_ffn, w_router, b_router, w_up, b_up, w_down, b_down, g_final):
    depth = w_in.shape[0]
    assert depth == 1, "one layer: the per-group outputs below are the final residual streams"
    bp, seq, d = x_prompt.shape
    nb, lq, _ = x_sample.shape
    assert bp == 1
    past = cache_sb_k.shape[2]
    band_keep = min(CB_LEFT_CHUNKS * CHUNK, seq)
    l = 0
    row2 = lambda a: a.reshape(1, -1)
    weights = (row2(g_attn[l]), w_in[l].astype(BF16), row2(b_gate[l]),
               w_br_sb[l].astype(BF16), w_br_cb[l].astype(BF16), w_br_mem[l].astype(BF16),
               w_out[l].astype(BF16), row2(g_ffn[l]), w_router[l].astype(BF16), row2(b_router[l]))

    mk, mv = _memkv(mem_prompt[0], row2(g_mem[l]), w_mem_kv[l].astype(BF16))

    def attend_prompt(qkv):
        return (_sb_prompt(qkv, _tile(seq, 256)), _cb_prompt(qkv, rel_table[l], _tile(seq, 256)),
                _xattn(qkv, mk, mv, _tile(seq, 512), per_tile_kv=False))

    def attend_sample(qkv):
        flat = lambda c: c.reshape(c.shape[0], c.shape[1] * c.shape[2], c.shape[3])
        return (_sb_sample(qkv, _seq_minor(cache_sb_k), _seq_minor(cache_sb_v), l, lq, _tile(past, 256)),
                _cb_sample(qkv, _seq_minor(cache_cb_k), _seq_minor(cache_cb_v), rel_table[l], l, lq),
                _xattn(qkv, flat(cache_mem_k[l]), flat(cache_mem_v[l]), lq, per_tile_kv=True))

    no_tokens = jnp.zeros((1, w_router.shape[2]), I32)
    routed_p, (ksb_p, vsb_p, kcb_p, vcb_p) = _attn_block(
        x_prompt.reshape(seq, d), attend_prompt, weights, no_tokens, _tile(seq, 512))
    routed_s, (ksb_s, vsb_s, kcb_s, vcb_s) = _attn_block(
        x_sample.reshape(nb * lq, d), attend_sample, weights, routed_p[5], _tile(nb * lq, 512))
    yp, ys = _moe([routed_p[:5], routed_s[:5]], routed_s[5], w_up[l], w_down[l],
                  b_up[l][:, None, 0::2], b_up[l][:, None, 1::2], b_down[l][:, None, :], row2(g_final))

    heads = lambda a, b, n, h, hd: a.reshape(1, b, n, h, hd)
    return (yp.reshape(1, seq, d), ys.reshape(nb, lq, d),
            heads(ksb_p, 1, seq, SB_HEADS, HEAD_DIM), heads(vsb_p, 1, seq, SB_HEADS, HEAD_DIM),
            heads(kcb_p[seq - band_keep:], 1, band_keep, CB_HEADS, HEAD_DIM),
            heads(vcb_p[seq - band_keep:], 1, band_keep, CB_HEADS, HEAD_DIM),
            heads(mk, 1, N_MEM, MEM_HEADS, MEM_HEAD_DIM), heads(mv, 1, N_MEM, MEM_HEADS, MEM_HEAD_DIM),
            heads(ksb_s, nb, lq, SB_HEADS, HEAD_DIM), heads(vsb_s, nb, lq, SB_HEADS, HEAD_DIM),
            heads(kcb_s, nb, lq, CB_HEADS, HEAD_DIM), heads(vcb_s, nb, lq, CB_HEADS, HEAD_DIM))
```

```python
import functools

import jax
import jax.numpy as jnp
import numpy as np
from jax import lax
from jax.experimental import pallas as pl
from jax.experimental.pallas import tpu as pltpu

F32 = jnp.float32
BF16 = jnp.bfloat16
I32 = jnp.int32

CHUNK = 64
N_MEM = 256
SB_HEADS = 8
CB_HEADS = 8
HEAD_DIM = 64
MEM_HEADS = 4
MEM_HEAD_DIM = 128
CB_LEFT_CHUNKS = 8
REL_CLIP = 128
ATT_W = 512
N_QKV_SEG = 7
N_EXPERTS = 32
TOP_K = 4
SWIGLU_LIMIT = 7.0
SWIGLU_ALPHA = 1.702
RMS_EPS = 1e-6
NEG_INF = -1e30

LANES = 128
VMEM_LIMIT = 56 * 1024 * 1024
SB_UNDERFLOW = 104.0
LOG2_E = 1.4426950408889634

EXPERT_ROWS = 256
CHUNK_DMA_PRIORITY = 1
DISPATCH_ROWS = 256
COMBINE_ROWS = 512
SB_PAIRS_PER_STEP = 4
ROW_UNROLL = 8

QK_SCALE = HEAD_DIM ** -0.5
MEM_SCALE = MEM_HEAD_DIM ** -0.5


def _cparams(sem):
    return pltpu.CompilerParams(dimension_semantics=sem, vmem_limit_bytes=VMEM_LIMIT)


def _rms(x, g):
    return x * lax.rsqrt(jnp.mean(x * x, axis=-1, keepdims=True) + RMS_EPS) * g


def _dot(a, b):
    return jnp.dot(a, b, preferred_element_type=F32)


def _dot_nt(a, b):
    return lax.dot_general(a, b, (((1,), (1,)), ((), ())), preferred_element_type=F32)


def _inproj_kernel(x_ref, g_ref, w_ref, bg_ref, qkv_ref, ksb_ref, vsb_ref, kcb_ref, vcb_ref, gate_ref, h_scr):
    h_scr[...] = _rms(x_ref[...], g_ref[...]).astype(BF16)
    f32_out = {1: ksb_ref, 2: vsb_ref, 4: kcb_ref, 5: vcb_ref}
    for seg in range(w_ref.shape[1] // ATT_W):
        cols = slice(seg * ATT_W, (seg + 1) * ATT_W)
        acc = _dot(h_scr[...], w_ref[:, cols])
        if seg < N_QKV_SEG:
            qkv_ref[:, cols] = (acc * (QK_SCALE if seg in (0, 3) else 1.0)).astype(BF16)
            if seg in f32_out:
                f32_out[seg][...] = acc
        else:
            gcols = slice((seg - N_QKV_SEG) * ATT_W, (seg - N_QKV_SEG + 1) * ATT_W)
            gate_ref[:, gcols] = jax.nn.sigmoid(acc + bg_ref[:, gcols]).astype(BF16)


def _resident(a):
    return pl.BlockSpec(a.shape, lambda *_: (0,) * a.ndim, pipeline_mode=pl.Buffered(1))


def _inproj(x, g, w_bf16, b_gate, tm):
    t, d = x.shape
    n_gate = w_bf16.shape[1] - N_QKV_SEG * ATT_W
    row = lambda i: (i, 0)
    f32_spec = pl.BlockSpec((tm, ATT_W), row)
    return pl.pallas_call(
        _inproj_kernel,
        grid=(t // tm,),
        in_specs=[pl.BlockSpec((tm, d), row), _resident(g), _resident(w_bf16), _resident(b_gate)],
        out_specs=[pl.BlockSpec((tm, N_QKV_SEG * ATT_W), row), f32_spec, f32_spec, f32_spec, f32_spec,
                   pl.BlockSpec((tm, n_gate), row)],
        out_shape=[jax.ShapeDtypeStruct((t, N_QKV_SEG * ATT_W), BF16)]
        + [jax.ShapeDtypeStruct((t, ATT_W), F32)] * 4
        + [jax.ShapeDtypeStruct((t, n_gate), BF16)],
        scratch_shapes=[pltpu.VMEM((tm, d), BF16)],
        compiler_params=_cparams(("parallel",)),
        name="inproj",
    )(x, g, w_bf16, b_gate)


def _head(a, h):
    return a[:, h * HEAD_DIM:(h + 1) * HEAD_DIM]


def _head_select(shape):
    return lax.broadcasted_iota(I32, shape, 1) < HEAD_DIM


def _suffix_ones(n):
    return (lax.broadcasted_iota(I32, (n, n), 0) > lax.broadcasted_iota(I32, (n, n), 1)).astype(BF16)


def _sb_block(q, kb, vb, c, u, causal=None, valid=None, seq_minor=False):
    z2 = (_dot(q, kb) if seq_minor else _dot_nt(q, kb)) * LOG2_E
    sp = jnp.maximum(z2, 0.0) + jnp.log2(1.0 + jnp.exp2(-jnp.abs(z2)))
    keep = causal if valid is None else (valid if causal is None else jnp.logical_and(causal, valid))
    l1m = -sp if keep is None else jnp.where(keep, -sp, 0.0)
    hi = l1m.astype(BF16)
    lo = (l1m - hi.astype(F32)).astype(BF16)
    tail = _dot(hi, u) + _dot(lo, u)
    a = jnp.exp2((z2 - sp) + tail + c)
    if keep is not None:
        a = jnp.where(keep, a, 0.0)
    pv = _dot_nt(a.astype(BF16), vb) if seq_minor else _dot(a.astype(BF16), vb)
    return pv, c + jnp.sum(l1m, axis=1, keepdims=True)


def _any_live(cs):
    return (functools.reduce(jnp.maximum, [jnp.max(c) for c in cs]) > -SB_UNDERFLOW * LOG2_E).astype(I32)


def _sb_older_blocks(first_block, fetch, step, cs, accs):
    n = len(cs)

    def cond(carry):
        return jnp.logical_and(carry[0] >= 0, carry[1] > 0)

    def body(carry):
        b, cs, accs = carry[0], carry[2:2 + n], carry[2 + n:]
        kb, vb = fetch(b)
        res = [step(s, kb, vb, cs[s]) for s in range(n)]
        cs = [r[1] for r in res]
        return (b - 1, _any_live(cs), *cs, *[acc + r[0] for acc, r in zip(accs, res)])

    out = lax.while_loop(cond, body, (first_block, _any_live(cs), *cs, *accs))
    return out[2 + n:]


def _fetch_blocks(copies):
    for c in copies:
        c.start()
    for c in copies:
        c.wait()


def _sb_prompt_kernel(q_ref, kd_ref, vd_ref, kp_ref, vp_ref, qkv_hbm, o_ref, kbuf, vbuf, sem, *, kcol, vcol):
    g, i = pl.program_id(0), pl.program_id(1)
    tq, w = q_ref.shape
    npair = w // LANES
    pair_lanes = lambda a, p: a[:, p * LANES:(p + 1) * LANES]
    first_head = _head_select((tq, LANES))
    causal = lax.broadcasted_iota(I32, (tq, tq), 1) < lax.broadcasted_iota(I32, (tq, tq), 0)
    u = _suffix_ones(tq)
    qs, cs, accs = [], [], []
    for p in range(npair):
        q = pair_lanes(q_ref, p)
        kd, vd, kp, vp = (pair_lanes(r, p) for r in (kd_ref, vd_ref, kp_ref, vp_ref))
        for qa in (jnp.where(first_head, q, jnp.zeros_like(q)), jnp.where(first_head, jnp.zeros_like(q), q)):
            pv_d, c = _sb_block(qa, kd, vd, jnp.zeros((tq, 1), F32), u, causal=causal)
            pv_p, c = _sb_block(qa, kp, vp, c, u, valid=i > 0)
            qs.append(qa)
            cs.append(c)
            accs.append(pv_d + pv_p)

    def fetch(b):
        rows = pl.ds(pl.multiple_of(b * tq, tq), tq)
        slab = lambda col: pl.ds(pl.multiple_of(col + g * w, LANES), w)
        _fetch_blocks([pltpu.make_async_copy(qkv_hbm.at[rows, slab(kcol)], kbuf, sem.at[0]),
                       pltpu.make_async_copy(qkv_hbm.at[rows, slab(vcol)], vbuf, sem.at[1])])
        return kbuf, vbuf

    def step(s, kb, vb, c):
        return _sb_block(qs[s], pair_lanes(kb, s // 2), pair_lanes(vb, s // 2), c, u)

    accs = _sb_older_blocks(i - 2, fetch, step, cs, accs)
    for p in range(npair):
        o_ref[:, p * LANES:(p + 1) * LANES] = jnp.where(first_head, accs[2 * p], accs[2 * p + 1]).astype(o_ref.dtype)


def _sb_prompt(qkv, tq):
    s = qkv.shape[0]
    w = SB_PAIRS_PER_STEP * LANES
    groups = ATT_W // w
    kcol, vcol = groups, 2 * groups
    prev = lambda col: pl.BlockSpec((tq, w), lambda g, i: (jnp.maximum(i - 1, 0), col + g))
    return pl.pallas_call(
        functools.partial(_sb_prompt_kernel, kcol=kcol * w, vcol=vcol * w),
        grid=(groups, s // tq),
        in_specs=[pl.BlockSpec((tq, w), lambda g, i: (i, g)),
                  pl.BlockSpec((tq, w), lambda g, i: (i, kcol + g)),
                  pl.BlockSpec((tq, w), lambda g, i: (i, vcol + g)),
                  prev(kcol), prev(vcol),
                  pl.BlockSpec(memory_space=pl.ANY)],
        out_specs=pl.BlockSpec((tq, w), lambda g, i: (i, g)),
        out_shape=jax.ShapeDtypeStruct((s, ATT_W), BF16),
        scratch_shapes=[pltpu.VMEM((tq, w), BF16), pltpu.VMEM((tq, w), BF16),
                        pltpu.SemaphoreType.DMA((2,))],
        compiler_params=_cparams(("arbitrary", "arbitrary")),
        name="sb_prompt",
    )(qkv, qkv, qkv, qkv, qkv, qkv)


def _sb_sample_kernel(q_ref, kn_ref, vn_ref, kp_ref, vp_ref, ck_hbm, cv_hbm, o_ref, kbuf, vbuf, sem, *, layer):
    batch = pl.program_id(0)
    lq = q_ref.shape[0]
    bk = kp_ref.shape[2]
    q, kn, vn = q_ref[...], kn_ref[...], vn_ref[...]
    causal = lax.broadcasted_iota(I32, (lq, lq), 1) < lax.broadcasted_iota(I32, (lq, lq), 0)
    u_new, u_old = _suffix_ones(lq), _suffix_ones(bk)
    qs = [_head(q, h) for h in range(SB_HEADS)]
    cs, accs = [], []
    for h in range(SB_HEADS):
        pv_n, c = _sb_block(qs[h], _head(kn, h), _head(vn, h), jnp.zeros((lq, 1), F32), u_new, causal=causal)
        pv_p, c = _sb_block(qs[h], kp_ref[h].astype(BF16), vp_ref[h].astype(BF16), c, u_old, seq_minor=True)
        cs.append(c)
        accs.append(pv_n + pv_p)

    def fetch(b):
        cols = pl.ds(pl.multiple_of(b * bk, bk), bk)
        _fetch_blocks([pltpu.make_async_copy(ck_hbm.at[layer, batch, :, :, cols], kbuf, sem.at[0]),
                       pltpu.make_async_copy(cv_hbm.at[layer, batch, :, :, cols], vbuf, sem.at[1])])
        return kbuf, vbuf

    def step(h, kb, vb, c):
        return _sb_block(qs[h], kb[h].astype(BF16), vb[h].astype(BF16), c, u_old, seq_minor=True)

    n_old = ck_hbm.shape[4] // bk
    accs = _sb_older_blocks(n_old - 2, fetch, step, cs, accs)
    o_ref[...] = jnp.concatenate(accs, axis=1).astype(o_ref.dtype)


def _seq_minor(cache):
    return jnp.transpose(cache, (0, 1, 3, 4, 2))


def _sb_sample(qkv, cache_k, cache_v, layer, lq, bk):
    _, nb, nh, hd, past = cache_k.shape
    assert past % bk == 0
    last = past // bk - 1
    newest = pl.BlockSpec((None, None, nh, hd, bk), lambda b: (layer, b, 0, 0, last))
    return pl.pallas_call(
        functools.partial(_sb_sample_kernel, layer=layer),
        grid=(nb,),
        in_specs=[pl.BlockSpec((lq, ATT_W), lambda b: (b, 0)),
                  pl.BlockSpec((lq, ATT_W), lambda b: (b, 1)),
                  pl.BlockSpec((lq, ATT_W), lambda b: (b, 2)),
                  newest, newest,
                  pl.BlockSpec(memory_space=pl.ANY), pl.BlockSpec(memory_space=pl.ANY)],
        out_specs=pl.BlockSpec((lq, ATT_W), lambda b: (b, 0)),
        out_shape=jax.ShapeDtypeStruct((nb * lq, ATT_W), BF16),
        scratch_shapes=[pltpu.VMEM((nh, hd, bk), cache_k.dtype), pltpu.VMEM((nh, hd, bk), cache_v.dtype),
                        pltpu.SemaphoreType.DMA((2,))],
        compiler_params=_cparams(("arbitrary",)),
        name="sb_sample",
    )(qkv, qkv, qkv, cache_k, cache_v, cache_k, cache_v)


def _softmax_pv(s_blocks, v_blocks):
    m = jnp.max(functools.reduce(jnp.maximum, s_blocks), axis=1, keepdims=True)
    p_blocks = [jnp.exp(s - m) for s in s_blocks]
    denom = jnp.sum(functools.reduce(jnp.add, p_blocks), axis=1, keepdims=True)
    pv = functools.reduce(jnp.add, [_dot(p.astype(BF16), v) for p, v in zip(p_blocks, v_blocks)])
    return pv / denom


def _cb_kernel(q_ref, *refs, nblk, lead_blocks):
    k_refs, v_refs = refs[:nblk], refs[nblk:2 * nblk]
    bias_ref, o_ref = refs[2 * nblk], refs[2 * nblk + 1]
    tq = q_ref.shape[0]
    first_head = _head_select((tq, LANES))
    for pair in range(CB_HEADS // 2):
        lanes = slice(pair * LANES, (pair + 1) * LANES)
        q = q_ref[:, lanes]
        ks = [r[:, lanes] for r in k_refs]
        vs = [r[:, lanes] for r in v_refs]
        outs = []
        for head in range(2):
            qa = jnp.where(first_head if head == 0 else ~first_head, q, jnp.zeros_like(q))
            s_blocks = []
            for n in range(nblk):
                b = bias_ref[2 * pair + head, :, n * tq:(n + 1) * tq]
                s = _dot_nt(qa, ks[n]) + b
                if n < lead_blocks:
                    s = jnp.where(pl.program_id(0) >= lead_blocks - n, s, NEG_INF)
                s_blocks.append(s)
            outs.append(_softmax_pv(s_blocks, vs))
        o_ref[:, lanes] = jnp.where(first_head, outs[0], outs[1]).astype(o_ref.dtype)


def _rel_bias(rel_table, lq, lk, q_off):
    m = lq + lk
    pos = jnp.arange(m, dtype=I32)
    j_minus_i = jnp.where(pos < lk, pos, pos - m)
    diag = rel_table[:, jnp.clip(q_off - j_minus_i, -REL_CLIP, REL_CLIP) + REL_CLIP].astype(F32)
    h = rel_table.shape[0]
    tiled = jnp.broadcast_to(diag[:, None, :], (h, lq, m)).reshape(h, lq * m)[:, :lq * (m - 1)]
    return tiled.reshape(h, lq, m - 1)[:, :, :lk]


def _cb_prompt(qkv, rel_table, tq):
    s = qkv.shape[0]
    lead = (CB_LEFT_CHUNKS * CHUNK) // tq
    nblk = lead + 1
    q_pos = lead * tq + jnp.arange(tq, dtype=I32)
    k_pos = jnp.arange(nblk * tq, dtype=I32)
    qc, kc = q_pos // CHUNK, k_pos // CHUNK
    band = (kc[None, :] >= qc[:, None] - CB_LEFT_CHUNKS) & (kc[None, :] <= qc[:, None])
    bias = jnp.where(band[None], _rel_bias(rel_table, tq, nblk * tq, lead * tq), NEG_INF)
    qcol = 3
    kv_specs = [pl.BlockSpec((tq, ATT_W), functools.partial(
        lambda i, back, col: (jnp.maximum(i - back, 0), col), back=lead - n, col=col))
        for col in (qcol + 1, qcol + 2) for n in range(nblk)]
    return pl.pallas_call(
        functools.partial(_cb_kernel, nblk=nblk, lead_blocks=lead),
        grid=(s // tq,),
        in_specs=[pl.BlockSpec((tq, ATT_W), lambda i: (i, qcol))] + kv_specs + [_resident(bias)],
        out_specs=pl.BlockSpec((tq, ATT_W), lambda i: (i, 0)),
        out_shape=jax.ShapeDtypeStruct((s, ATT_W), BF16),
        compiler_params=_cparams(("parallel",)),
        name="cb_prompt",
    )(qkv, *([qkv] * (2 * nblk)), bias)


def _cb_sample_kernel(q_ref, kn_ref, vn_ref, ck_ref, cv_ref, bias_ref, o_ref):
    q, kn, vn = q_ref[...], kn_ref[...], vn_ref[...]
    cb_past = ck_ref.shape[2]
    outs = []
    for h in range(CB_HEADS):
        qh = _head(q, h)
        s_old = _dot(qh, ck_ref[h].astype(BF16)) + bias_ref[h, :, :cb_past]
        s_new = _dot_nt(qh, _head(kn, h)) + bias_ref[h, :, cb_past:]
        m = jnp.maximum(jnp.max(s_old, axis=1, keepdims=True), jnp.max(s_new, axis=1, keepdims=True))
        p_old, p_new = jnp.exp(s_old - m), jnp.exp(s_new - m)
        denom = jnp.sum(p_old, axis=1, keepdims=True) + jnp.sum(p_new, axis=1, keepdims=True)
        pv = _dot_nt(p_old.astype(BF16), cv_ref[h].astype(BF16)) + _dot(p_new.astype(BF16), _head(vn, h))
        outs.append(pv / denom)
    o_ref[...] = jnp.concatenate(outs, axis=1).astype(o_ref.dtype)


def _cb_sample(qkv, cache_k, cache_v, rel_table, layer, lq):
    _, nb, nh, hd, cb_past = cache_k.shape
    bias = _rel_bias(rel_table, lq, cb_past + lq, cb_past)
    cache_spec = pl.BlockSpec((None, None, nh, hd, cb_past), lambda b: (layer, b, 0, 0, 0))
    qcol = 3
    return pl.pallas_call(
        _cb_sample_kernel,
        grid=(nb,),
        in_specs=[pl.BlockSpec((lq, ATT_W), lambda b: (b, qcol)),
                  pl.BlockSpec((lq, ATT_W), lambda b: (b, qcol + 1)),
                  pl.BlockSpec((lq, ATT_W), lambda b: (b, qcol + 2)),
                  cache_spec, cache_spec, _resident(bias)],
        out_specs=pl.BlockSpec((lq, ATT_W), lambda b: (b, 0)),
        out_shape=jax.ShapeDtypeStruct((nb * lq, ATT_W), BF16),
        compiler_params=_cparams(("parallel",)),
        name="cb_sample",
    )(qkv, qkv, qkv, cache_k, cache_v, bias)


def _memkv_kernel(mem_ref, g_ref, w_ref, k_ref, v_ref):
    h = _rms(mem_ref[...], g_ref[...]).astype(BF16)
    kv = _dot(h, w_ref[...])
    half = k_ref.shape[1]
    k_ref[...] = kv[:, :half]
    v_ref[...] = kv[:, half:]


def _memkv(mem, g, w_bf16):
    n = mem.shape[0]
    half = w_bf16.shape[1] // 2
    return pl.pallas_call(
        _memkv_kernel,
        out_shape=[jax.ShapeDtypeStruct((n, half), F32)] * 2,
        compiler_params=pltpu.CompilerParams(vmem_limit_bytes=VMEM_LIMIT),
        name="memkv",
    )(mem, g, w_bf16)


def _xattn_kernel(q_ref, k_ref, v_ref, o_ref, *, rows_by_head):
    q = q_ref[...]
    for h in range(MEM_HEADS):
        cols = slice(h * MEM_HEAD_DIM, (h + 1) * MEM_HEAD_DIM)
        if rows_by_head:
            rows = pl.ds(h, k_ref.shape[0] // MEM_HEADS, stride=MEM_HEADS)
            k, v = k_ref[rows, :].astype(BF16), v_ref[rows, :].astype(BF16)
        else:
            k, v = k_ref[:, cols].astype(BF16), v_ref[:, cols].astype(BF16)
        s = _dot_nt(q[:, cols], k) * MEM_SCALE
        o_ref[:, cols] = _softmax_pv([s], [v]).astype(o_ref.dtype)


def _xattn(qkv, mem_k, mem_v, tq, per_tile_kv):
    t = qkv.shape[0]
    qcol = (N_QKV_SEG - 1)
    if per_tile_kv:
        kv_spec = pl.BlockSpec((None,) + mem_k.shape[1:], lambda i: (i, 0, 0))
    else:
        kv_spec = pl.BlockSpec(mem_k.shape, lambda i: (0, 0))
    return pl.pallas_call(
        functools.partial(_xattn_kernel, rows_by_head=per_tile_kv),
        grid=(t // tq,),
        in_specs=[pl.BlockSpec((tq, ATT_W), lambda i: (i, qcol)), kv_spec, kv_spec],
        out_specs=pl.BlockSpec((tq, ATT_W), lambda i: (i, 0)),
        out_shape=jax.ShapeDtypeStruct((t, ATT_W), BF16),
        compiler_params=_cparams(("parallel",)),
        name="xattn",
    )(qkv, mem_k, mem_v)


def _store_row_tiles(ref, x):
    for j in range(ref.shape[-2]):
        ref[:, j, :] = x[:, j * LANES:(j + 1) * LANES]


def _load_row_tiles(ref):
    return jnp.concatenate([ref[:, j, :] for j in range(ref.shape[-2])], axis=1)


def _merge_kernel(x_ref, osb_ref, ocb_ref, om_ref, g_ref, wsb_ref, wcb_ref, wm_ref, wo_ref,
                  gf_ref, wr_ref, br_ref, cnt0_ref,
                  y_ref, hf_ref, e_ref, gate_ref, rank_ref, cnt_ref, cnt_scr):
    i = pl.program_id(0)
    tm, d = x_ref.shape

    @pl.when(i == 0)
    def _():
        cnt_scr[...] = cnt0_ref[...].astype(F32)

    m = None
    for n, (o_ref, w_ref) in enumerate(((osb_ref, wsb_ref), (ocb_ref, wcb_ref), (om_ref, wm_ref))):
        term = g_ref[:, n * d:(n + 1) * d].astype(F32) * _dot(o_ref[...], w_ref[...])
        m = term if m is None else m + term
    y = x_ref[...] + _dot(m.astype(BF16), wo_ref[...])
    y_ref[...] = y
    hf = _rms(y, gf_ref[...])
    _store_row_tiles(hf_ref, hf)
    logits = _dot(hf.astype(BF16), wr_ref[...]) + br_ref[...]

    ne = logits.shape[1]
    lane = lax.broadcasted_iota(I32, (tm, ne), 1)
    work = logits
    vals, sels, idxs = [], [], []
    for _ in range(TOP_K):
        mx = jnp.max(work, axis=1, keepdims=True)
        idx = jnp.min(jnp.where(work == mx, lane, ne), axis=1, keepdims=True)
        sel = lane == idx
        vals.append(mx)
        idxs.append(idx)
        sels.append(sel)
        work = jnp.where(sel, -jnp.inf, work)
    ex = [jnp.exp(v - vals[0]) for v in vals]
    denom = functools.reduce(jnp.add, ex)

    onehot = functools.reduce(jnp.add, [s.astype(F32) for s in sels])
    before = (lax.broadcasted_iota(I32, (tm, tm), 1) < lax.broadcasted_iota(I32, (tm, tm), 0)).astype(BF16)
    prior = _dot(before, onehot.astype(BF16)) + cnt_scr[...]
    for k in range(TOP_K):
        e_ref[:, k:k + 1] = idxs[k]
        gate_ref[:, k:k + 1] = ex[k] / denom
        rank_ref[:, k:k + 1] = jnp.sum(jnp.where(sels[k], prior, 0.0), axis=1, keepdims=True).astype(I32)
    cnt_scr[...] += jnp.sum(onehot, axis=0, keepdims=True)
    cnt_ref[...] = cnt_scr[...].astype(I32)


def _merge(x, o_sb, o_cb, o_m, gates, w_sb, w_cb, w_m, w_out, g_ffn, w_router, b_router, cnt0, tm):
    t, d = x.shape
    ne = w_router.shape[1]
    row = lambda i: (i, 0)
    return pl.pallas_call(
        _merge_kernel,
        grid=(t // tm,),
        in_specs=[pl.BlockSpec((tm, d), row)] + [pl.BlockSpec((tm, ATT_W), row)] * 3
        + [pl.BlockSpec((tm, gates.shape[1]), row)]
        + [_resident(a) for a in (w_sb, w_cb, w_m, w_out, g_ffn, w_router, b_router, cnt0)],
        out_specs=[pl.BlockSpec((tm, d), row), pl.BlockSpec((tm, d // LANES, LANES), lambda i: (i, 0, 0))]
        + [pl.BlockSpec((tm, TOP_K), row)] * 3 + [pl.BlockSpec((1, ne), lambda i: (0, 0))],
        out_shape=[jax.ShapeDtypeStruct((t, d), F32), jax.ShapeDtypeStruct((t, d // LANES, LANES), F32),
                   jax.ShapeDtypeStruct((t, TOP_K), I32), jax.ShapeDtypeStruct((t, TOP_K), F32),
                   jax.ShapeDtypeStruct((t, TOP_K), I32), jax.ShapeDtypeStruct((1, ne), I32)],
        scratch_shapes=[pltpu.VMEM((1, ne), F32)],
        compiler_params=_cparams(("arbitrary",)),
        name="merge_router",
    )(x, o_sb, o_cb, o_m, gates, w_sb, w_cb, w_m, w_out, g_ffn, w_router, b_router, cnt0)


def _row_copies(make_copy, n_rows):
    def start(r, carry):
        for k in range(TOP_K):
            make_copy(r, k).start(priority=k % 2)
        return carry

    def wait(r, carry):
        for k in range(TOP_K):
            make_copy(r, k).wait()
        return carry

    lax.fori_loop(0, n_rows, start, 0, unroll=ROW_UNROLL)
    lax.fori_loop(0, n_rows, wait, 0, unroll=ROW_UNROLL)


def _n_chunks(count):
    return lax.shift_right_logical(count + (EXPERT_ROWS - 1), EXPERT_ROWS.bit_length() - 1)


def _used_rows(start_ref, cnt_ref):
    last = cnt_ref.shape[0] - 1
    return start_ref[last] + _n_chunks(cnt_ref[last]) * EXPERT_ROWS


def _zero_chunks(zeros_ref, dst_hbm, sem, n, chunk_row):
    def copy(c):
        rows = pl.ds(pl.multiple_of(chunk_row(c), EXPERT_ROWS), EXPERT_ROWS)
        return pltpu.make_async_copy(zeros_ref, dst_hbm.at[rows], sem)

    def start(c, carry):
        copy(c).start()
        return carry

    def wait(c, carry):
        copy(c).wait()
        return carry

    lax.fori_loop(0, n, start, 0)
    lax.fori_loop(0, n, wait, 0)


def _zero_tail(zeros_ref, dst_hbm, sem, start_ref, cnt_ref):
    used = _used_rows(start_ref, cnt_ref)
    n_tail = lax.shift_right_logical(dst_hbm.shape[0] - used, EXPERT_ROWS.bit_length() - 1)
    _zero_chunks(zeros_ref, dst_hbm, sem, n_tail, lambda c: used + c * EXPERT_ROWS)


def _dispatch_kernel(dest_ref, start_ref, cnt_ref, *refs, tiles):
    hf_refs, (xs_ref, sem, zbuf, zsem) = refs[:len(tiles)], refs[len(tiles):]
    i = pl.program_id(0)
    tm = hf_refs[0].shape[0]

    @pl.when(i == 0)
    def _():
        zbuf[...] = jnp.zeros_like(zbuf)
        last_row = xs_ref.shape[0] - EXPERT_ROWS

        def last_chunk(e):
            return jnp.minimum(start_ref[e] + jnp.maximum(_n_chunks(cnt_ref[e]) - 1, 0) * EXPERT_ROWS, last_row)

        _zero_chunks(zbuf, xs_ref, zsem, cnt_ref.shape[0], last_chunk)
        _zero_tail(zbuf, xs_ref, zsem, start_ref, cnt_ref)

    first_tile = 0
    for hf_ref, n_tiles in zip(hf_refs, tiles):
        @pl.when(jnp.logical_and(i >= first_tile, i < first_tile + n_tiles))
        def _(hf_ref=hf_ref):
            def row_copy(r, k):
                dst = dest_ref[(i * tm + r) * TOP_K + k]
                return pltpu.make_async_copy(hf_ref.at[pl.ds(r, 1)], xs_ref.at[pl.ds(dst, 1)], sem)

            _row_copies(row_copy, tm)
        first_tile += n_tiles


def _dispatch(dest_flat, row_start, counts, hfs, rows, tm):
    row = hfs[0].shape[1:]
    tiles = tuple(hf.shape[0] // tm for hf in hfs)
    firsts = np.concatenate([[0], np.cumsum(tiles)[:-1]])
    specs = [pl.BlockSpec((tm,) + row, functools.partial(
        lambda i, *_, first, n: (jnp.clip(i - first, 0, n - 1), 0, 0), first=int(f), n=n))
        for f, n in zip(firsts, tiles)]
    return pl.pallas_call(
        functools.partial(_dispatch_kernel, tiles=tiles),
        grid_spec=pltpu.PrefetchScalarGridSpec(
            num_scalar_prefetch=3,
            grid=(sum(tiles),),
            in_specs=specs,
            out_specs=pl.BlockSpec(memory_space=pl.ANY),
            scratch_shapes=[pltpu.SemaphoreType.DMA, pltpu.VMEM((EXPERT_ROWS,) + row, hfs[0].dtype),
                            pltpu.SemaphoreType.DMA]),
        out_shape=jax.ShapeDtypeStruct((rows,) + row, hfs[0].dtype),
        compiler_params=_cparams(("arbitrary",)),
        name="moe_dispatch",
    )(dest_flat, row_start, counts, *hfs)


def _expert_kernel(start_ref, cnt_ref, xs_hbm, wup_ref, wd_ref, bg_ref, bu_ref, bd_ref, ys_hbm,
                   wg_scr, wu_scr, wd_scr, xbuf, ybuf, sem_in, sem_out):
    e = pl.program_id(0)
    row0 = start_ref[e]
    n = _n_chunks(cnt_ref[e])

    def rows(j):
        return pl.ds(pl.multiple_of(row0 + j * EXPERT_ROWS, EXPERT_ROWS), EXPERT_ROWS)

    class in_copy:
        def __init__(self, j, slot):
            self.copies = [pltpu.make_async_copy(xs_hbm.at[rows(j), c, :],
                                                 xbuf.at[slot, :, pl.ds(c * LANES, LANES)], sem_in.at[slot])
                           for c in range(xs_hbm.shape[1])]

        def start(self, priority):
            for c in self.copies:
                c.start(priority=priority)

        def wait(self):
            for c in self.copies:
                c.wait()

    def out_copy(j, slot):
        return pltpu.make_async_copy(ybuf.at[slot], ys_hbm.at[rows(j)], sem_out.at[slot])

    @pl.when(n > 0)
    def _():
        in_copy(0, 0).start(priority=CHUNK_DMA_PRIORITY)
        grp = 2 * LANES
        src = lax.broadcasted_iota(I32, (grp, grp), 0)
        dst = lax.broadcasted_iota(I32, (grp, grp), 1)
        perm = (src == jnp.where(dst < LANES, 2 * dst, 2 * (dst - LANES) + 1)).astype(BF16)
        for c in range(wup_ref.shape[1] // grp):
            w = _dot(wup_ref[:, c * grp:(c + 1) * grp].astype(BF16), perm).astype(BF16)
            wg_scr[:, c * LANES:(c + 1) * LANES] = w[:, :LANES]
            wu_scr[:, c * LANES:(c + 1) * LANES] = w[:, LANES:]
        wd_scr[...] = wd_ref[...].astype(BF16)

        def chunk(j, carry):
            slot = lax.rem(j, 2)
            in_copy(j, slot).wait()

            @pl.when(j + 1 < n)
            def _():
                in_copy(j + 1, 1 - slot).start(priority=CHUNK_DMA_PRIORITY)

            @pl.when(j >= 2)
            def _():
                out_copy(j - 2, slot).wait()

            x = xbuf[slot].astype(BF16)
            g = jnp.minimum(_dot(x, wg_scr[...]) + bg_ref[...], SWIGLU_LIMIT)
            u = jnp.clip(_dot(x, wu_scr[...]) + bu_ref[...], -SWIGLU_LIMIT, SWIGLU_LIMIT)
            act = (u + 1.0) * (g * jax.nn.sigmoid(SWIGLU_ALPHA * g))
            ybuf[slot] = _dot(act.astype(BF16), wd_scr[...]) + bd_ref[...]
            out_copy(j, slot).start(priority=CHUNK_DMA_PRIORITY)
            return carry

        lax.fori_loop(0, n, chunk, 0)

        @pl.when(n >= 2)
        def _():
            out_copy(n - 2, lax.rem(n, 2)).wait()

        out_copy(n - 1, lax.rem(n - 1, 2)).wait()

    @pl.when(e == pl.num_programs(0) - 1)
    def _():
        ybuf[0] = jnp.zeros(ybuf.shape[1:], ybuf.dtype)
        _zero_tail(ybuf.at[0], ys_hbm, sem_out.at[0], start_ref, cnt_ref)


def _experts(row_start, counts, xs, w_up, w_d, b_g, b_u, b_d):
    rows = xs.shape[0]
    ne, f, d = w_d.shape
    wspec = lambda a: pl.BlockSpec((None,) + a.shape[1:], lambda e, *_: (e, 0, 0))
    hbm = pl.BlockSpec(memory_space=pl.ANY)
    return pl.pallas_call(
        _expert_kernel,
        grid_spec=pltpu.PrefetchScalarGridSpec(
            num_scalar_prefetch=2,
            grid=(ne,),
            in_specs=[hbm] + [wspec(a) for a in (w_up, w_d, b_g, b_u, b_d)],
            out_specs=hbm,
            scratch_shapes=[pltpu.VMEM((d, f), BF16), pltpu.VMEM((d, f), BF16), pltpu.VMEM((f, d), BF16),
                            pltpu.VMEM((2, EXPERT_ROWS, d), xs.dtype), pltpu.VMEM((2, EXPERT_ROWS, d), F32),
                            pltpu.SemaphoreType.DMA((2,)), pltpu.SemaphoreType.DMA((2,))]),
        out_shape=jax.ShapeDtypeStruct((rows, d), F32),
        compiler_params=_cparams(("arbitrary",)),
        name="moe_experts",
    )(row_start, counts, xs, w_up, w_d, b_g, b_u, b_d)


def _combine_kernel(dest_ref, y_ref, gate_ref, gfin_ref, ys_ref, o_ref, buf, sem):
    i = pl.program_id(0)
    tm = y_ref.shape[0]

    def row_copy(r, k):
        src = dest_ref[(i * tm + r) * TOP_K + k]
        return pltpu.make_async_copy(ys_ref.at[pl.ds(src, 1)], buf.at[k, pl.ds(r, 1)], sem)

    _row_copies(row_copy, tm)
    moe = functools.reduce(jnp.add, [buf[k] * gate_ref[:, k:k + 1] for k in range(TOP_K)])
    o_ref[...] = _rms(y_ref[...] + moe, gfin_ref[...])


def _combine(dest_flat, y, gate, g_final, ys, tm):
    t, d = y.shape
    return pl.pallas_call(
        _combine_kernel,
        grid_spec=pltpu.PrefetchScalarGridSpec(
            num_scalar_prefetch=1,
            grid=(t // tm,),
            in_specs=[pl.BlockSpec((tm, d), lambda i, dest: (i, 0)),
                      pl.BlockSpec((tm, TOP_K), lambda i, dest: (i, 0)),
                      pl.BlockSpec((1, d), lambda i, dest: (0, 0)),
                      pl.BlockSpec(memory_space=pl.ANY)],
            out_specs=pl.BlockSpec((tm, d), lambda i, dest: (i, 0)),
            scratch_shapes=[pltpu.VMEM((TOP_K, tm, d), F32), pltpu.SemaphoreType.DMA]),
        out_shape=jax.ShapeDtypeStruct((t, d), F32),
        compiler_params=_cparams(("arbitrary",)),
        name="moe_combine",
    )(dest_flat, y, gate, g_final, ys)


def _common_tile(sizes, pref):
    t = pref
    while any(s % t for s in sizes):
        t //= 2
    return t


def _moe(groups, counts, w_up, w_d, b_g, b_u, b_d, g_final):
    ne = counts.shape[1]
    n_pairs = sum(g[0].shape[0] for g in groups) * TOP_K
    rows = (n_pairs + ne * (EXPERT_ROWS - 1) + EXPERT_ROWS - 1) // EXPERT_ROWS * EXPERT_ROWS
    counts = counts[0]
    chunks = (counts + EXPERT_ROWS - 1) // EXPERT_ROWS
    eid = jnp.arange(ne, dtype=I32)
    row_start = jnp.sum(jnp.where(eid[None, :] < eid[:, None], chunks[None, :], 0), axis=1) * EXPERT_ROWS
    dests = [(jnp.sum(jnp.where(top_e[..., None] == eid, row_start, 0), axis=-1) + rank).reshape(-1)
             for _, _, top_e, _, rank in groups]
    sizes = [g[0].shape[0] for g in groups]
    xs = _dispatch(jnp.concatenate(dests), row_start, counts, [g[1] for g in groups], rows,
                   _common_tile(sizes, DISPATCH_ROWS))
    ys = _experts(row_start, counts, xs, w_up, w_d, b_g, b_u, b_d)
    return [_combine(dest, y, gate, g_final, ys, _common_tile([y.shape[0]], COMBINE_ROWS))
            for (y, _, _, gate, _), dest in zip(groups, dests)]


def _tile(n, pref):
    return pref if n % pref == 0 else n


def _attn_block(x, attend, weights, cnt0, tm):
    g_attn, w_in, b_gate, w_sb, w_cb, w_m, w_out, g_ffn, w_router, b_router = weights
    qkv, ksb, vsb, kcb, vcb, gates = _inproj(x, g_attn, w_in, b_gate, tm)
    o_sb, o_cb, o_m = attend(qkv)
    routed = _merge(x, o_sb, o_cb, o_m, gates, w_sb, w_cb, w_m, w_out, g_ffn, w_router, b_router, cnt0, tm)
    return routed, (ksb, vsb, kcb, vcb)


def kernel(x_prompt, x_sample, mem_prompt, cache_sb_k, cache_sb_v, cache_cb_k, cache_cb_v, cache_mem_k, cache_mem_v, g_attn, w_in, b_gate, rel_table, g_mem, w_mem_kv, w_br_sb, w_br_cb, w_br_mem, w_out, g_ffn, w_router, b_router, w_up, b_up, w_down, b_down, g_final):
    depth = w_in.shape[0]
    assert depth == 1, "one layer: the per-group outputs below are the final residual streams"
    bp, seq, d = x_prompt.shape
    nb, lq, _ = x_sample.shape
    assert bp == 1
    past = cache_sb_k.shape[2]
    band_keep = min(CB_LEFT_CHUNKS * CHUNK, seq)
    l = 0
    row2 = lambda a: a.reshape(1, -1)
    weights = (row2(g_attn[l]), w_in[l].astype(BF16), row2(b_gate[l]),
               w_br_sb[l].astype(BF16), w_br_cb[l].astype(BF16), w_br_mem[l].astype(BF16),
               w_out[l].astype(BF16), row2(g_ffn[l]), w_router[l].astype(BF16), row2(b_router[l]))

    mk, mv = _memkv(mem_prompt[0], row2(g_mem[l]), w_mem_kv[l].astype(BF16))

    def attend_prompt(qkv):
        return (_sb_prompt(qkv, _tile(seq, 256)), _cb_prompt(qkv, rel_table[l], _tile(seq, 256)),
                _xattn(qkv, mk, mv, _tile(seq, 512), per_tile_kv=False))

    def attend_sample(qkv):
        flat = lambda c: c.reshape(c.shape[0], c.shape[1] * c.shape[2], c.shape[3])
        return (_sb_sample(qkv, _seq_minor(cache_sb_k), _seq_minor(cache_sb_v), l, lq, _tile(past, 256)),
                _cb_sample(qkv, _seq_minor(cache_cb_k), _seq_minor(cache_cb_v), rel_table[l], l, lq),
                _xattn(qkv, flat(cache_mem_k[l]), flat(cache_mem_v[l]), lq, per_tile_kv=True))

    no_tokens = jnp.zeros((1, w_router.shape[2]), I32)
    routed_p, (ksb_p, vsb_p, kcb_p, vcb_p) = _attn_block(
        x_prompt.reshape(seq, d), attend_prompt, weights, no_tokens, _tile(seq, 512))
    routed_s, (ksb_s, vsb_s, kcb_s, vcb_s) = _attn_block(
        x_sample.reshape(nb * lq, d), attend_sample, weights, routed_p[5], _tile(nb * lq, 512))
    yp, ys = _moe([routed_p[:5], routed_s[:5]], routed_s[5], w_up[l], w_down[l],
                  b_up[l][:, None, 0::2], b_up[l][:, None, 1::2], b_down[l][:, None, :], row2(g_final))

    heads = lambda a, b, n, h, hd: a.reshape(1, b, n, h, hd)
    return (yp.reshape(1, seq, d), ys.reshape(nb, lq, d),
            heads(ksb_p, 1, seq, SB_HEADS, HEAD_DIM), heads(vsb_p, 1, seq, SB_HEADS, HEAD_DIM),
            heads(kcb_p[seq - band_keep:], 1, band_keep, CB_HEADS, HEAD_DIM),
            heads(vcb_p[seq - band_keep:], 1, band_keep, CB_HEADS, HEAD_DIM),
            heads(mk, 1, N_MEM, MEM_HEADS, MEM_HEAD_DIM), heads(mv, 1, N_MEM, MEM_HEADS, MEM_HEAD_DIM),
            heads(ksb_s, nb, lq, SB_HEADS, HEAD_DIM), heads(vsb_s, nb, lq, SB_HEADS, HEAD_DIM),
            heads(kcb_s, nb, lq, CB_HEADS, HEAD_DIM), heads(vcb_s, nb, lq, CB_HEADS, HEAD_DIM))
```

```python
import functools

import jax
import jax.numpy as jnp
import numpy as np
from jax import lax
from jax.experimental import pallas as pl
from jax.experimental.pallas import tpu as pltpu

F32 = jnp.float32
BF16 = jnp.bfloat16
I32 = jnp.int32

CHUNK = 64
N_MEM = 256
SB_HEADS = 8
CB_HEADS = 8
HEAD_DIM = 64
MEM_HEADS = 4
MEM_HEAD_DIM = 128
CB_LEFT_CHUNKS = 8
REL_CLIP = 128
ATT_W = 512
N_QKV_SEG = 7
N_EXPERTS = 32
TOP_K = 4
SWIGLU_LIMIT = 7.0
SWIGLU_ALPHA = 1.702
RMS_EPS = 1e-6
NEG_INF = -1e30

LANES = 128
VMEM_LIMIT = 56 * 1024 * 1024
SB_UNDERFLOW = 104.0
LOG2_E = 1.4426950408889634

EXPERT_ROWS = 256
CHUNK_DMA_PRIORITY = 1
DISPATCH_ROWS = 256
COMBINE_ROWS = 512
SB_PAIRS_PER_STEP = 4
ROW_UNROLL = 8

QK_SCALE = HEAD_DIM ** -0.5
MEM_SCALE = MEM_HEAD_DIM ** -0.5


def _cparams(sem):
    return pltpu.CompilerParams(dimension_semantics=sem, vmem_limit_bytes=VMEM_LIMIT)


def _rms(x, g):
    return x * lax.rsqrt(jnp.mean(x * x, axis=-1, keepdims=True) + RMS_EPS) * g


def _dot(a, b):
    return jnp.dot(a, b, preferred_element_type=F32)


def _dot_nt(a, b):
    return lax.dot_general(a, b, (((1,), (1,)), ((), ())), preferred_element_type=F32)


def _inproj_kernel(x_ref, g_ref, w_ref, bg_ref, qkv_ref, ksb_ref, vsb_ref, kcb_ref, vcb_ref, gate_ref, h_scr):
    h_scr[...] = _rms(x_ref[...], g_ref[...]).astype(BF16)
    f32_out = {1: ksb_ref, 2: vsb_ref, 4: kcb_ref, 5: vcb_ref}
    for seg in range(w_ref.shape[1] // ATT_W):
        cols = slice(seg * ATT_W, (seg + 1) * ATT_W)
        acc = _dot(h_scr[...], w_ref[:, cols])
        if seg < N_QKV_SEG:
            qkv_ref[:, cols] = (acc * (QK_SCALE if seg in (0, 3) else 1.0)).astype(BF16)
            if seg in f32_out:
                f32_out[seg][...] = acc
        else:
            gcols = slice((seg - N_QKV_SEG) * ATT_W, (seg - N_QKV_SEG + 1) * ATT_W)
            gate_ref[:, gcols] = jax.nn.sigmoid(acc + bg_ref[:, gcols]).astype(BF16)


def _resident(a):
    return pl.BlockSpec(a.shape, lambda *_: (0,) * a.ndim, pipeline_mode=pl.Buffered(1))


def _inproj(x, g, w_bf16, b_gate, tm):
    t, d = x.shape
    n_gate = w_bf16.shape[1] - N_QKV_SEG * ATT_W
    row = lambda i: (i, 0)
    f32_spec = pl.BlockSpec((tm, ATT_W), row)
    return pl.pallas_call(
        _inproj_kernel,
        grid=(t // tm,),
        in_specs=[pl.BlockSpec((tm, d), row), _resident(g), _resident(w_bf16), _resident(b_gate)],
        out_specs=[pl.BlockSpec((tm, N_QKV_SEG * ATT_W), row), f32_spec, f32_spec, f32_spec, f32_spec,
                   pl.BlockSpec((tm, n_gate), row)],
        out_shape=[jax.ShapeDtypeStruct((t, N_QKV_SEG * ATT_W), BF16)]
        + [jax.ShapeDtypeStruct((t, ATT_W), F32)] * 4
        + [jax.ShapeDtypeStruct((t, n_gate), BF16)],
        scratch_shapes=[pltpu.VMEM((tm, d), BF16)],
        compiler_params=_cparams(("parallel",)),
        name="inproj",
    )(x, g, w_bf16, b_gate)


def _head(a, h):
    return a[:, h * HEAD_DIM:(h + 1) * HEAD_DIM]


def _head_select(shape):
    return lax.broadcasted_iota(I32, shape, 1) < HEAD_DIM


def _suffix_ones(n):
    return (lax.broadcasted_iota(I32, (n, n), 0) > lax.broadcasted_iota(I32, (n, n), 1)).astype(BF16)


def _sb_block(q, kb, vb, c, u, causal=None, valid=None, seq_minor=False):
    z2 = (_dot(q, kb) if seq_minor else _dot_nt(q, kb)) * LOG2_E
    sp = jnp.maximum(z2, 0.0) + jnp.log2(1.0 + jnp.exp2(-jnp.abs(z2)))
    keep = causal if valid is None else (valid if causal is None else jnp.logical_and(causal, valid))
    l1m = -sp if keep is None else jnp.where(keep, -sp, 0.0)
    hi = l1m.astype(BF16)
    lo = (l1m - hi.astype(F32)).astype(BF16)
    tail = _dot(hi, u) + _dot(lo, u)
    a = jnp.exp2((z2 - sp) + tail + c)
    if keep is not None:
        a = jnp.where(keep, a, 0.0)
    pv = _dot_nt(a.astype(BF16), vb) if seq_minor else _dot(a.astype(BF16), vb)
    return pv, c + jnp.sum(l1m, axis=1, keepdims=True)


def _any_live(cs):
    return (functools.reduce(jnp.maximum, [jnp.max(c) for c in cs]) > -SB_UNDERFLOW * LOG2_E).astype(I32)


def _sb_older_blocks(first_block, fetch, step, cs, accs):
    n = len(cs)

    def cond(carry):
        return jnp.logical_and(carry[0] >= 0, carry[1] > 0)

    def body(carry):
        b, cs, accs = carry[0], carry[2:2 + n], carry[2 + n:]
        kb, vb = fetch(b)
        res = [step(s, kb, vb, cs[s]) for s in range(n)]
        cs = [r[1] for r in res]
        return (b - 1, _any_live(cs), *cs, *[acc + r[0] for acc, r in zip(accs, res)])

    out = lax.while_loop(cond, body, (first_block, _any_live(cs), *cs, *accs))
    return out[2 + n:]


def _fetch_blocks(copies):
    for c in copies:
        c.start()
    for c in copies:
        c.wait()


def _sb_prompt_kernel(q_ref, kd_ref, vd_ref, kp_ref, vp_ref, qkv_hbm, o_ref, kbuf, vbuf, sem, *, kcol, vcol):
    g, i = pl.program_id(0), pl.program_id(1)
    tq, w = q_ref.shape
    npair = w // LANES
    pair_lanes = lambda a, p: a[:, p * LANES:(p + 1) * LANES]
    first_head = _head_select((tq, LANES))
    causal = lax.broadcasted_iota(I32, (tq, tq), 1) < lax.broadcasted_iota(I32, (tq, tq), 0)
    u = _suffix_ones(tq)
    qs, cs, accs = [], [], []
    for p in range(npair):
        q = pair_lanes(q_ref, p)
        kd, vd, kp, vp = (pair_lanes(r, p) for r in (kd_ref, vd_ref, kp_ref, vp_ref))
        for qa in (jnp.where(first_head, q, jnp.zeros_like(q)), jnp.where(first_head, jnp.zeros_like(q), q)):
            pv_d, c = _sb_block(qa, kd, vd, jnp.zeros((tq, 1), F32), u, causal=causal)
            pv_p, c = _sb_block(qa, kp, vp, c, u, valid=i > 0)
            qs.append(qa)
            cs.append(c)
            accs.append(pv_d + pv_p)

    def fetch(b):
        rows = pl.ds(pl.multiple_of(b * tq, tq), tq)
        slab = lambda col: pl.ds(pl.multiple_of(col + g * w, LANES), w)
        _fetch_blocks([pltpu.make_async_copy(qkv_hbm.at[rows, slab(kcol)], kbuf, sem.at[0]),
                       pltpu.make_async_copy(qkv_hbm.at[rows, slab(vcol)], vbuf, sem.at[1])])
        return kbuf, vbuf

    def step(s, kb, vb, c):
        return _sb_block(qs[s], pair_lanes(kb, s // 2), pair_lanes(vb, s // 2), c, u)

    accs = _sb_older_blocks(i - 2, fetch, step, cs, accs)
    for p in range(npair):
        o_ref[:, p * LANES:(p + 1) * LANES] = jnp.where(first_head, accs[2 * p], accs[2 * p + 1]).astype(o_ref.dtype)


def _sb_prompt(qkv, tq):
    s = qkv.shape[0]
    w = SB_PAIRS_PER_STEP * LANES
    groups = ATT_W // w
    kcol, vcol = groups, 2 * groups
    prev = lambda col: pl.BlockSpec((tq, w), lambda g, i: (jnp.maximum(i - 1, 0), col + g))
    return pl.pallas_call(
        functools.partial(_sb_prompt_kernel, kcol=kcol * w, vcol=vcol * w),
        grid=(groups, s // tq),
        in_specs=[pl.BlockSpec((tq, w), lambda g, i: (i, g)),
                  pl.BlockSpec((tq, w), lambda g, i: (i, kcol + g)),
                  pl.BlockSpec((tq, w), lambda g, i: (i, vcol + g)),
                  prev(kcol), prev(vcol),
                  pl.BlockSpec(memory_space=pl.ANY)],
        out_specs=pl.BlockSpec((tq, w), lambda g, i: (i, g)),
        out_shape=jax.ShapeDtypeStruct((s, ATT_W), BF16),
        scratch_shapes=[pltpu.VMEM((tq, w), BF16), pltpu.VMEM((tq, w), BF16),
                        pltpu.SemaphoreType.DMA((2,))],
        compiler_params=_cparams(("arbitrary", "arbitrary")),
        name="sb_prompt",
    )(qkv, qkv, qkv, qkv, qkv, qkv)


def _sb_sample_kernel(q_ref, kn_ref, vn_ref, kp_ref, vp_ref, ck_hbm, cv_hbm, o_ref, kbuf, vbuf, sem, *, layer):
    batch = pl.program_id(0)
    lq = q_ref.shape[0]
    bk = kp_ref.shape[2]
    q, kn, vn = q_ref[...], kn_ref[...], vn_ref[...]
    causal = lax.broadcasted_iota(I32, (lq, lq), 1) < lax.broadcasted_iota(I32, (lq, lq), 0)
    u_new, u_old = _suffix_ones(lq), _suffix_ones(bk)
    qs = [_head(q, h) for h in range(SB_HEADS)]
    cs, accs = [], []
    for h in range(SB_HEADS):
        pv_n, c = _sb_block(qs[h], _head(kn, h), _head(vn, h), jnp.zeros((lq, 1), F32), u_new, causal=causal)
        pv_p, c = _sb_block(qs[h], kp_ref[h].astype(BF16), vp_ref[h].astype(BF16), c, u_old, seq_minor=True)
        cs.append(c)
        accs.append(pv_n + pv_p)

    def fetch(b):
        cols = pl.ds(pl.multiple_of(b * bk, bk), bk)
        _fetch_blocks([pltpu.make_async_copy(ck_hbm.at[layer, batch, :, :, cols], kbuf, sem.at[0]),
                       pltpu.make_async_copy(cv_hbm.at[layer, batch, :, :, cols], vbuf, sem.at[1])])
        return kbuf, vbuf

    def step(h, kb, vb, c):
        return _sb_block(qs[h], kb[h].astype(BF16), vb[h].astype(BF16), c, u_old, seq_minor=True)

    n_old = ck_hbm.shape[4] // bk
    accs = _sb_older_blocks(n_old - 2, fetch, step, cs, accs)
    o_ref[...] = jnp.concatenate(accs, axis=1).astype(o_ref.dtype)


def _seq_minor(cache):
    return jnp.transpose(cache, (0, 1, 3, 4, 2))


def _sb_sample(qkv, cache_k, cache_v, layer, lq, bk):
    _, nb, nh, hd, past = cache_k.shape
    assert past % bk == 0
    last = past // bk - 1
    newest = pl.BlockSpec((None, None, nh, hd, bk), lambda b: (layer, b, 0, 0, last))
    return pl.pallas_call(
        functools.partial(_sb_sample_kernel, layer=layer),
        grid=(nb,),
        in_specs=[pl.BlockSpec((lq, ATT_W), lambda b: (b, 0)),
                  pl.BlockSpec((lq, ATT_W), lambda b: (b, 1)),
                  pl.BlockSpec((lq, ATT_W), lambda b: (b, 2)),
                  newest, newest,
                  pl.BlockSpec(memory_space=pl.ANY), pl.BlockSpec(memory_space=pl.ANY)],
        out_specs=pl.BlockSpec((lq, ATT_W), lambda b: (b, 0)),
        out_shape=jax.ShapeDtypeStruct((nb * lq, ATT_W), BF16),
        scratch_shapes=[pltpu.VMEM((nh, hd, bk), cache_k.dtype), pltpu.VMEM((nh, hd, bk), cache_v.dtype),
                        pltpu.SemaphoreType.DMA((2,))],
        compiler_params=_cparams(("arbitrary",)),
        name="sb_sample",
    )(qkv, qkv, qkv, cache_k, cache_v, cache_k, cache_v)


def _softmax_pv(s_blocks, v_blocks):
    m = jnp.max(functools.reduce(jnp.maximum, s_blocks), axis=1, keepdims=True)
    p_blocks = [jnp.exp(s - m) for s in s_blocks]
    denom = jnp.sum(functools.reduce(jnp.add, p_blocks), axis=1, keepdims=True)
    pv = functools.reduce(jnp.add, [_dot(p.astype(BF16), v) for p, v in zip(p_blocks, v_blocks)])
    return pv / denom


def _cb_kernel(q_ref, *refs, nblk, lead_blocks):
    k_refs, v_refs = refs[:nblk], refs[nblk:2 * nblk]
    bias_ref, o_ref = refs[2 * nblk], refs[2 * nblk + 1]
    tq = q_ref.shape[0]
    first_head = _head_select((tq, LANES))
    for pair in range(CB_HEADS // 2):
        lanes = slice(pair * LANES, (pair + 1) * LANES)
        q = q_ref[:, lanes]
        ks = [r[:, lanes] for r in k_refs]
        vs = [r[:, lanes] for r in v_refs]
        outs = []
        for head in range(2):
            qa = jnp.where(first_head if head == 0 else ~first_head, q, jnp.zeros_like(q))
            s_blocks = []
            for n in range(nblk):
                b = bias_ref[2 * pair + head, :, n * tq:(n + 1) * tq]
                s = _dot_nt(qa, ks[n]) + b
                if n < lead_blocks:
                    s = jnp.where(pl.program_id(0) >= lead_blocks - n, s, NEG_INF)
                s_blocks.append(s)
            outs.append(_softmax_pv(s_blocks, vs))
        o_ref[:, lanes] = jnp.where(first_head, outs[0], outs[1]).astype(o_ref.dtype)


def _rel_bias(rel_table, lq, lk, q_off):
    m = lq + lk
    pos = jnp.arange(m, dtype=I32)
    j_minus_i = jnp.where(pos < lk, pos, pos - m)
    diag = rel_table[:, jnp.clip(q_off - j_minus_i, -REL_CLIP, REL_CLIP) + REL_CLIP].astype(F32)
    h = rel_table.shape[0]
    tiled = jnp.broadcast_to(diag[:, None, :], (h, lq, m)).reshape(h, lq * m)[:, :lq * (m - 1)]
    return tiled.reshape(h, lq, m - 1)[:, :, :lk]


def _cb_prompt(qkv, rel_table, tq):
    s = qkv.shape[0]
    lead = (CB_LEFT_CHUNKS * CHUNK) // tq
    nblk = lead + 1
    q_pos = lead * tq + jnp.arange(tq, dtype=I32)
    k_pos = jnp.arange(nblk * tq, dtype=I32)
    qc, kc = q_pos // CHUNK, k_pos // CHUNK
    band = (kc[None, :] >= qc[:, None] - CB_LEFT_CHUNKS) & (kc[None, :] <= qc[:, None])
    bias = jnp.where(band[None], _rel_bias(rel_table, tq, nblk * tq, lead * tq), NEG_INF)
    qcol = 3
    kv_specs = [pl.BlockSpec((tq, ATT_W), functools.partial(
        lambda i, back, col: (jnp.maximum(i - back, 0), col), back=lead - n, col=col))
        for col in (qcol + 1, qcol + 2) for n in range(nblk)]
    return pl.pallas_call(
        functools.partial(_cb_kernel, nblk=nblk, lead_blocks=lead),
        grid=(s // tq,),
        in_specs=[pl.BlockSpec((tq, ATT_W), lambda i: (i, qcol))] + kv_specs + [_resident(bias)],
        out_specs=pl.BlockSpec((tq, ATT_W), lambda i: (i, 0)),
        out_shape=jax.ShapeDtypeStruct((s, ATT_W), BF16),
        compiler_params=_cparams(("parallel",)),
        name="cb_prompt",
    )(qkv, *([qkv] * (2 * nblk)), bias)


def _cb_sample_kernel(q_ref, kn_ref, vn_ref, ck_ref, cv_ref, bias_ref, o_ref):
    q, kn, vn = q_ref[...], kn_ref[...], vn_ref[...]
    cb_past = ck_ref.shape[2]
    outs = []
    for h in range(CB_HEADS):
        qh = _head(q, h)
        s_old = _dot(qh, ck_ref[h].astype(BF16)) + bias_ref[h, :, :cb_past]
        s_new = _dot_nt(qh, _head(kn, h)) + bias_ref[h, :, cb_past:]
        m = jnp.maximum(jnp.max(s_old, axis=1, keepdims=True), jnp.max(s_new, axis=1, keepdims=True))
        p_old, p_new = jnp.exp(s_old - m), jnp.exp(s_new - m)
        denom = jnp.sum(p_old, axis=1, keepdims=True) + jnp.sum(p_new, axis=1, keepdims=True)
        pv = _dot_nt(p_old.astype(BF16), cv_ref[h].astype(BF16)) + _dot(p_new.astype(BF16), _head(vn, h))
        outs.append(pv / denom)
    o_ref[...] = jnp.concatenate(outs, axis=1).astype(o_ref.dtype)


def _cb_sample(qkv, cache_k, cache_v, rel_table, layer, lq):
    _, nb, nh, hd, cb_past = cache_k.shape
    bias = _rel_bias(rel_table, lq, cb_past + lq, cb_past)
    cache_spec = pl.BlockSpec((None, None, nh, hd, cb_past), lambda b: (layer, b, 0, 0, 0))
    qcol = 3
    return pl.pallas_call(
        _cb_sample_kernel,
        grid=(nb,),
        in_specs=[pl.BlockSpec((lq, ATT_W), lambda b: (b, qcol)),
                  pl.BlockSpec((lq, ATT_W), lambda b: (b, qcol + 1)),
                  pl.BlockSpec((lq, ATT_W), lambda b: (b, qcol + 2)),
                  cache_spec, cache_spec, _resident(bias)],
        out_specs=pl.BlockSpec((lq, ATT_W), lambda b: (b, 0)),
        out_shape=jax.ShapeDtypeStruct((nb * lq, ATT_W), BF16),
        compiler_params=_cparams(("parallel",)),
        name="cb_sample",
    )(qkv, qkv, qkv, cache_k, cache_v, bias)


def _memkv_kernel(mem_ref, g_ref, w_ref, k_ref, v_ref):
    h = _rms(mem_ref[...], g_ref[...]).astype(BF16)
    kv = _dot(h, w_ref[...])
    half = k_ref.shape[1]
    k_ref[...] = kv[:, :half]
    v_ref[...] = kv[:, half:]


def _memkv(mem, g, w_bf16):
    n = mem.shape[0]
    half = w_bf16.shape[1] // 2
    return pl.pallas_call(
        _memkv_kernel,
        out_shape=[jax.ShapeDtypeStruct((n, half), F32)] * 2,
        compiler_params=pltpu.CompilerParams(vmem_limit_bytes=VMEM_LIMIT),
        name="memkv",
    )(mem, g, w_bf16)


def _xattn_kernel(q_ref, k_ref, v_ref, o_ref, *, rows_by_head):
    q = q_ref[...]
    for h in range(MEM_HEADS):
        cols = slice(h * MEM_HEAD_DIM, (h + 1) * MEM_HEAD_DIM)
        if rows_by_head:
            rows = pl.ds(h, k_ref.shape[0] // MEM_HEADS, stride=MEM_HEADS)
            k, v = k_ref[rows, :].astype(BF16), v_ref[rows, :].astype(BF16)
        else:
            k, v = k_ref[:, cols].astype(BF16), v_ref[:, cols].astype(BF16)
        s = _dot_nt(q[:, cols], k) * MEM_SCALE
        o_ref[:, cols] = _softmax_pv([s], [v]).astype(o_ref.dtype)


def _xattn(qkv, mem_k, mem_v, tq, per_tile_kv):
    t = qkv.shape[0]
    qcol = (N_QKV_SEG - 1)
    if per_tile_kv:
        kv_spec = pl.BlockSpec((None,) + mem_k.shape[1:], lambda i: (i, 0, 0))
    else:
        kv_spec = pl.BlockSpec(mem_k.shape, lambda i: (0, 0))
    return pl.pallas_call(
        functools.partial(_xattn_kernel, rows_by_head=per_tile_kv),
        grid=(t // tq,),
        in_specs=[pl.BlockSpec((tq, ATT_W), lambda i: (i, qcol)), kv_spec, kv_spec],
        out_specs=pl.BlockSpec((tq, ATT_W), lambda i: (i, 0)),
        out_shape=jax.ShapeDtypeStruct((t, ATT_W), BF16),
        compiler_params=_cparams(("parallel",)),
        name="xattn",
    )(qkv, mem_k, mem_v)


def _store_row_tiles(ref, x):
    for j in range(ref.shape[-2]):
        ref[:, j, :] = x[:, j * LANES:(j + 1) * LANES]


def _merge_kernel(x_ref, osb_ref, ocb_ref, om_ref, g_ref, wsb_ref, wcb_ref, wm_ref, wo_ref,
                  gf_ref, wr_ref, br_ref, cnt0_ref,
                  y_ref, hf_ref, e_ref, gate_ref, rank_ref, cnt_ref, cnt_scr):
    i = pl.program_id(0)
    tm, d = x_ref.shape

    @pl.when(i == 0)
    def _():
        cnt_scr[...] = cnt0_ref[...].astype(F32)

    m = None
    for n, (o_ref, w_ref) in enumerate(((osb_ref, wsb_ref), (ocb_ref, wcb_ref), (om_ref, wm_ref))):
        term = g_ref[:, n * d:(n + 1) * d].astype(F32) * _dot(o_ref[...], w_ref[...])
        m = term if m is None else m + term
    y = x_ref[...] + _dot(m.astype(BF16), wo_ref[...])
    y_ref[...] = y
    hf = _rms(y, gf_ref[...])
    _store_row_tiles(hf_ref, hf)
    logits = _dot(hf.astype(BF16), wr_ref[...]) + br_ref[...]

    ne = logits.shape[1]
    lane = lax.broadcasted_iota(I32, (tm, ne), 1)
    work = logits
    vals, sels, idxs = [], [], []
    for _ in range(TOP_K):
        mx = jnp.max(work, axis=1, keepdims=True)
        idx = jnp.min(jnp.where(work == mx, lane, ne), axis=1, keepdims=True)
        sel = lane == idx
        vals.append(mx)
        idxs.append(idx)
        sels.append(sel)
        work = jnp.where(sel, -jnp.inf, work)
    ex = [jnp.exp(v - vals[0]) for v in vals]
    denom = functools.reduce(jnp.add, ex)

    onehot = functools.reduce(jnp.add, [s.astype(F32) for s in sels])
    before = (lax.broadcasted_iota(I32, (tm, tm), 1) < lax.broadcasted_iota(I32, (tm, tm), 0)).astype(BF16)
    prior = _dot(before, onehot.astype(BF16)) + cnt_scr[...]
    for k in range(TOP_K):
        e_ref[:, k:k + 1] = idxs[k]
        gate_ref[:, k:k + 1] = ex[k] / denom
        rank_ref[:, k:k + 1] = jnp.sum(jnp.where(sels[k], prior, 0.0), axis=1, keepdims=True).astype(I32)
    cnt_scr[...] += jnp.sum(onehot, axis=0, keepdims=True)
    cnt_ref[...] = cnt_scr[...].astype(I32)


def _merge(x, o_sb, o_cb, o_m, gates, w_sb, w_cb, w_m, w_out, g_ffn, w_router, b_router, cnt0, tm):
    t, d = x.shape
    ne = w_router.shape[1]
    row = lambda i: (i, 0)
    return pl.pallas_call(
        _merge_kernel,
        grid=(t // tm,),
        in_specs=[pl.BlockSpec((tm, d), row)] + [pl.BlockSpec((tm, ATT_W), row)] * 3
        + [pl.BlockSpec((tm, gates.shape[1]), row)]
        + [_resident(a) for a in (w_sb, w_cb, w_m, w_out, g_ffn, w_router, b_router, cnt0)],
        out_specs=[pl.BlockSpec((tm, d), row), pl.BlockSpec((tm, d // LANES, LANES), lambda i: (i, 0, 0))]
        + [pl.BlockSpec((tm, TOP_K), row)] * 3 + [pl.BlockSpec((1, ne), lambda i: (0, 0))],
        out_shape=[jax.ShapeDtypeStruct((t, d), F32), jax.ShapeDtypeStruct((t, d // LANES, LANES), F32),
                   jax.ShapeDtypeStruct((t, TOP_K), I32), jax.ShapeDtypeStruct((t, TOP_K), F32),
                   jax.ShapeDtypeStruct((t, TOP_K), I32), jax.ShapeDtypeStruct((1, ne), I32)],
        scratch_shapes=[pltpu.VMEM((1, ne), F32)],
        compiler_params=_cparams(("arbitrary",)),
        name="merge_router",
    )(x, o_sb, o_cb, o_m, gates, w_sb, w_cb, w_m, w_out, g_ffn, w_router, b_router, cnt0)


def _row_copies(make_copy, n_rows):
    def start(r, carry):
        for k in range(TOP_K):
            make_copy(r, k).start(priority=k % 2)
        return carry

    def wait(r, carry):
        for k in range(TOP_K):
            make_copy(r, k).wait()
        return carry

    lax.fori_loop(0, n_rows, start, 0, unroll=ROW_UNROLL)
    lax.fori_loop(0, n_rows, wait, 0, unroll=ROW_UNROLL)


def _n_chunks(count):
    return lax.shift_right_logical(count + (EXPERT_ROWS - 1), EXPERT_ROWS.bit_length() - 1)


def _used_rows(start_ref, cnt_ref):
    last = cnt_ref.shape[0] - 1
    return start_ref[last] + _n_chunks(cnt_ref[last]) * EXPERT_ROWS


def _zero_chunks(zeros_ref, dst_hbm, sem, n, chunk_row):
    def copy(c):
        rows = pl.ds(pl.multiple_of(chunk_row(c), EXPERT_ROWS), EXPERT_ROWS)
        return pltpu.make_async_copy(zeros_ref, dst_hbm.at[rows], sem)

    def start(c, carry):
        copy(c).start()
        return carry

    def wait(c, carry):
        copy(c).wait()
        return carry

    lax.fori_loop(0, n, start, 0)
    lax.fori_loop(0, n, wait, 0)


def _zero_tail(zeros_ref, dst_hbm, sem, start_ref, cnt_ref):
    used = _used_rows(start_ref, cnt_ref)
    n_tail = lax.shift_right_logical(dst_hbm.shape[0] - used, EXPERT_ROWS.bit_length() - 1)
    _zero_chunks(zeros_ref, dst_hbm, sem, n_tail, lambda c: used + c * EXPERT_ROWS)


def _dispatch_kernel(dest_ref, start_ref, cnt_ref, *refs, tiles):
    hf_refs, (xs_ref, sem, zbuf, zsem) = refs[:len(tiles)], refs[len(tiles):]
    i = pl.program_id(0)
    tm = hf_refs[0].shape[0]

    @pl.when(i == 0)
    def _():
        zbuf[...] = jnp.zeros_like(zbuf)
        last_row = xs_ref.shape[0] - EXPERT_ROWS

        def last_chunk(e):
            return jnp.minimum(start_ref[e] + jnp.maximum(_n_chunks(cnt_ref[e]) - 1, 0) * EXPERT_ROWS, last_row)

        _zero_chunks(zbuf, xs_ref, zsem, cnt_ref.shape[0], last_chunk)
        _zero_tail(zbuf, xs_ref, zsem, start_ref, cnt_ref)

    first_tile = 0
    for hf_ref, n_tiles in zip(hf_refs, tiles):
        @pl.when(jnp.logical_and(i >= first_tile, i < first_tile + n_tiles))
        def _(hf_ref=hf_ref):
            def row_copy(r, k):
                dst = dest_ref[(i * tm + r) * TOP_K + k]
                return pltpu.make_async_copy(hf_ref.at[pl.ds(r, 1)], xs_ref.at[pl.ds(dst, 1)], sem)

            _row_copies(row_copy, tm)
        first_tile += n_tiles


def _dispatch(dest_flat, row_start, counts, hfs, rows, tm):
    row = hfs[0].shape[1:]
    tiles = tuple(hf.shape[0] // tm for hf in hfs)
    firsts = np.concatenate([[0], np.cumsum(tiles)[:-1]])
    specs = [pl.BlockSpec((tm,) + row, functools.partial(
        lambda i, *_, first, n: (jnp.clip(i - first, 0, n - 1), 0, 0), first=int(f), n=n))
        for f, n in zip(firsts, tiles)]
    return pl.pallas_call(
        functools.partial(_dispatch_kernel, tiles=tiles),
        grid_spec=pltpu.PrefetchScalarGridSpec(
            num_scalar_prefetch=3,
            grid=(sum(tiles),),
            in_specs=specs,
            out_specs=pl.BlockSpec(memory_space=pl.ANY),
            scratch_shapes=[pltpu.SemaphoreType.DMA, pltpu.VMEM((EXPERT_ROWS,) + row, hfs[0].dtype),
                            pltpu.SemaphoreType.DMA]),
        out_shape=jax.ShapeDtypeStruct((rows,) + row, hfs[0].dtype),
        compiler_params=_cparams(("arbitrary",)),
        name="moe_dispatch",
    )(dest_flat, row_start, counts, *hfs)


def _expert_kernel(start_ref, cnt_ref, xs_hbm, wup_ref, wd_ref, bg_ref, bu_ref, bd_ref, ys_hbm,
                   wg_scr, wu_scr, wd_scr, xbuf, ybuf, sem_in, sem_out):
    e = pl.program_id(0)
    row0 = start_ref[e]
    n = _n_chunks(cnt_ref[e])

    def rows(j):
        return pl.ds(pl.multiple_of(row0 + j * EXPERT_ROWS, EXPERT_ROWS), EXPERT_ROWS)

    def in_copies(j, slot):
        return [pltpu.make_async_copy(xs_hbm.at[rows(j), c, :], xbuf.at[slot, :, pl.ds(c * LANES, LANES)],
                                      sem_in.at[slot]) for c in range(xs_hbm.shape[1])]

    def start_in(j, slot):
        for c in in_copies(j, slot):
            c.start(priority=CHUNK_DMA_PRIORITY)

    def wait_in(j, slot):
        for c in in_copies(j, slot):
            c.wait()

    def out_copy(j, slot):
        return pltpu.make_async_copy(ybuf.at[slot], ys_hbm.at[rows(j)], sem_out.at[slot])

    @pl.when(n > 0)
    def _():
        start_in(0, 0)
        grp = 2 * LANES
        src = lax.broadcasted_iota(I32, (grp, grp), 0)
        dst = lax.broadcasted_iota(I32, (grp, grp), 1)
        perm = (src == jnp.where(dst < LANES, 2 * dst, 2 * (dst - LANES) + 1)).astype(BF16)
        for c in range(wup_ref.shape[1] // grp):
            w = _dot(wup_ref[:, c * grp:(c + 1) * grp].astype(BF16), perm).astype(BF16)
            wg_scr[:, c * LANES:(c + 1) * LANES] = w[:, :LANES]
            wu_scr[:, c * LANES:(c + 1) * LANES] = w[:, LANES:]
        wd_scr[...] = wd_ref[...].astype(BF16)

        def chunk(j, carry):
            slot = lax.rem(j, 2)
            wait_in(j, slot)

            @pl.when(j + 1 < n)
            def _():
                start_in(j + 1, 1 - slot)

            @pl.when(j >= 2)
            def _():
                out_copy(j - 2, slot).wait()

            x = xbuf[slot].astype(BF16)
            g = jnp.minimum(_dot(x, wg_scr[...]) + bg_ref[...], SWIGLU_LIMIT)
            u = jnp.clip(_dot(x, wu_scr[...]) + bu_ref[...], -SWIGLU_LIMIT, SWIGLU_LIMIT)
            act = (u + 1.0) * (g * jax.nn.sigmoid(SWIGLU_ALPHA * g))
            ybuf[slot] = _dot(act.astype(BF16), wd_scr[...]) + bd_ref[...]
            out_copy(j, slot).start(priority=CHUNK_DMA_PRIORITY)
            return carry

        lax.fori_loop(0, n, chunk, 0)

        @pl.when(n >= 2)
        def _():
            out_copy(n - 2, lax.rem(n, 2)).wait()

        out_copy(n - 1, lax.rem(n - 1, 2)).wait()

    @pl.when(e == pl.num_programs(0) - 1)
    def _():
        ybuf[0] = jnp.zeros(ybuf.shape[1:], ybuf.dtype)
        _zero_tail(ybuf.at[0], ys_hbm, sem_out.at[0], start_ref, cnt_ref)


def _experts(row_start, counts, xs, w_up, w_d, b_g, b_u, b_d):
    rows = xs.shape[0]
    ne, f, d = w_d.shape
    wspec = lambda a: pl.BlockSpec((None,) + a.shape[1:], lambda e, *_: (e, 0, 0))
    hbm = pl.BlockSpec(memory_space=pl.ANY)
    return pl.pallas_call(
        _expert_kernel,
        grid_spec=pltpu.PrefetchScalarGridSpec(
            num_scalar_prefetch=2,
            grid=(ne,),
            in_specs=[hbm] + [wspec(a) for a in (w_up, w_d, b_g, b_u, b_d)],
            out_specs=hbm,
            scratch_shapes=[pltpu.VMEM((d, f), BF16), pltpu.VMEM((d, f), BF16), pltpu.VMEM((f, d), BF16),
                            pltpu.VMEM((2, EXPERT_ROWS, d), xs.dtype), pltpu.VMEM((2, EXPERT_ROWS, d), F32),
                            pltpu.SemaphoreType.DMA((2,)), pltpu.SemaphoreType.DMA((2,))]),
        out_shape=jax.ShapeDtypeStruct((rows, d), F32),
        compiler_params=_cparams(("arbitrary",)),
        name="moe_experts",
    )(row_start, counts, xs, w_up, w_d, b_g, b_u, b_d)


def _combine_kernel(dest_ref, y_ref, gate_ref, gfin_ref, ys_ref, o_ref, buf, sem):
    i = pl.program_id(0)
    tm = y_ref.shape[0]

    def row_copy(r, k):
        src = dest_ref[(i * tm + r) * TOP_K + k]
        return pltpu.make_async_copy(ys_ref.at[pl.ds(src, 1)], buf.at[k, pl.ds(r, 1)], sem)

    _row_copies(row_copy, tm)
    moe = functools.reduce(jnp.add, [buf[k] * gate_ref[:, k:k + 1] for k in range(TOP_K)])
    o_ref[...] = _rms(y_ref[...] + moe, gfin_ref[...])


def _combine(dest_flat, y, gate, g_final, ys, tm):
    t, d = y.shape
    return pl.pallas_call(
        _combine_kernel,
        grid_spec=pltpu.PrefetchScalarGridSpec(
            num_scalar_prefetch=1,
            grid=(t // tm,),
            in_specs=[pl.BlockSpec((tm, d), lambda i, dest: (i, 0)),
                      pl.BlockSpec((tm, TOP_K), lambda i, dest: (i, 0)),
                      pl.BlockSpec((1, d), lambda i, dest: (0, 0)),
                      pl.BlockSpec(memory_space=pl.ANY)],
            out_specs=pl.BlockSpec((tm, d), lambda i, dest: (i, 0)),
            scratch_shapes=[pltpu.VMEM((TOP_K, tm, d), F32), pltpu.SemaphoreType.DMA]),
        out_shape=jax.ShapeDtypeStruct((t, d), F32),
        compiler_params=_cparams(("arbitrary",)),
        name="moe_combine",
    )(dest_flat, y, gate, g_final, ys)


def _common_tile(sizes, pref):
    t = pref
    while any(s % t for s in sizes):
        t //= 2
    return t


def _moe(groups, counts, w_up, w_d, b_g, b_u, b_d, g_final):
    ne = counts.shape[1]
    n_pairs = sum(g[0].shape[0] for g in groups) * TOP_K
    rows = (n_pairs + ne * (EXPERT_ROWS - 1) + EXPERT_ROWS - 1) // EXPERT_ROWS * EXPERT_ROWS
    counts = counts[0]
    chunks = (counts + EXPERT_ROWS - 1) // EXPERT_ROWS
    eid = jnp.arange(ne, dtype=I32)
    row_start = jnp.sum(jnp.where(eid[None, :] < eid[:, None], chunks[None, :], 0), axis=1) * EXPERT_ROWS
    dests = [(jnp.sum(jnp.where(top_e[..., None] == eid, row_start, 0), axis=-1) + rank).reshape(-1)
             for _, _, top_e, _, rank in groups]
    sizes = [g[0].shape[0] for g in groups]
    xs = _dispatch(jnp.concatenate(dests), row_start, counts, [g[1] for g in groups], rows,
                   _common_tile(sizes, DISPATCH_ROWS))
    ys = _experts(row_start, counts, xs, w_up, w_d, b_g, b_u, b_d)
    return [_combine(dest, y, gate, g_final, ys, _common_tile([y.shape[0]], COMBINE_ROWS))
            for (y, _, _, gate, _), dest in zip(groups, dests)]


def _tile(n, pref):
    return pref if n % pref == 0 else n


def _attn_block(x, attend, weights, cnt0, tm):
    g_attn, w_in, b_gate, w_sb, w_cb, w_m, w_out, g_ffn, w_router, b_router = weights
    qkv, ksb, vsb, kcb, vcb, gates = _inproj(x, g_attn, w_in, b_gate, tm)
    o_sb, o_cb, o_m = attend(qkv)
    routed = _merge(x, o_sb, o_cb, o_m, gates, w_sb, w_cb, w_m, w_out, g_ffn, w_router, b_router, cnt0, tm)
    return routed, (ksb, vsb, kcb, vcb)


def kernel(x_prompt, x_sample, mem_prompt, cache_sb_k, cache_sb_v, cache_cb_k, cache_cb_v, cache_mem_k, cache_mem_v, g_attn, w_in, b_gate, rel_table, g_mem, w_mem_kv, w_br_sb, w_br_cb, w_br_mem, w_out, g_ffn, w_router, b_router, w_up, b_up, w_down, b_down, g_final):
    depth = w_in.shape[0]
    assert depth == 1, "one layer: the per-group outputs below are the final residual streams"
    bp, seq, d = x_prompt.shape
    nb, lq, _ = x_sample.shape
    assert bp == 1
    past = cache_sb_k.shape[2]
    band_keep = min(CB_LEFT_CHUNKS * CHUNK, seq)
    l = 0
    row2 = lambda a: a.reshape(1, -1)
    weights = (row2(g_attn[l]), w_in[l].astype(BF16), row2(b_gate[l]),
               w_br_sb[l].astype(BF16), w_br_cb[l].astype(BF16), w_br_mem[l].astype(BF16),
               w_out[l].astype(BF16), row2(g_ffn[l]), w_router[l].astype(BF16), row2(b_router[l]))

    mk, mv = _memkv(mem_prompt[0], row2(g_mem[l]), w_mem_kv[l].astype(BF16))

    def attend_prompt(qkv):
        return (_sb_prompt(qkv, _tile(seq, 256)), _cb_prompt(qkv, rel_table[l], _tile(seq, 256)),
                _xattn(qkv, mk, mv, _tile(seq, 512), per_tile_kv=False))

    def attend_sample(qkv):
        flat = lambda c: c.reshape(c.shape[0], c.shape[1] * c.shape[2], c.shape[3])
        return (_sb_sample(qkv, _seq_minor(cache_sb_k), _seq_minor(cache_sb_v), l, lq, _tile(past, 256)),
                _cb_sample(qkv, _seq_minor(cache_cb_k), _seq_minor(cache_cb_v), rel_table[l], l, lq),
                _xattn(qkv, flat(cache_mem_k[l]), flat(cache_mem_v[l]), lq, per_tile_kv=True))

    no_tokens = jnp.zeros((1, w_router.shape[2]), I32)
    routed_p, (ksb_p, vsb_p, kcb_p, vcb_p) = _attn_block(
        x_prompt.reshape(seq, d), attend_prompt, weights, no_tokens, _tile(seq, 512))
    routed_s, (ksb_s, vsb_s, kcb_s, vcb_s) = _attn_block(
        x_sample.reshape(nb * lq, d), attend_sample, weights, routed_p[5], _tile(nb * lq, 512))
    yp, ys = _moe([routed_p[:5], routed_s[:5]], routed_s[5], w_up[l], w_down[l],
                  b_up[l][:, None, 0::2], b_up[l][:, None, 1::2], b_down[l][:, None, :], row2(g_final))

    heads = lambda a, b, n, h, hd: a.reshape(1, b, n, h, hd)
    return (yp.reshape(1, seq, d), ys.reshape(nb, lq, d),
            heads(ksb_p, 1, seq, SB_HEADS, HEAD_DIM), heads(vsb_p, 1, seq, SB_HEADS, HEAD_DIM),
            heads(kcb_p[seq - band_keep:], 1, band_keep, CB_HEADS, HEAD_DIM),
            heads(vcb_p[seq - band_keep:], 1, band_keep, CB_HEADS, HEAD_DIM),
            heads(mk, 1, N_MEM, MEM_HEADS, MEM_HEAD_DIM), heads(mv, 1, N_MEM, MEM_HEADS, MEM_HEAD_DIM),
            heads(ksb_s, nb, lq, SB_HEADS, HEAD_DIM), heads(vsb_s, nb, lq, SB_HEADS, HEAD_DIM),
            heads(kcb_s, nb, lq, CB_HEADS, HEAD_DIM), heads(vcb_s, nb, lq, CB_HEADS, HEAD_DIM))
```
